```python
import jax, jax.numpy as jnp
from jax import lax
import numpy as np

D_MODEL = 1024
BATCH = 8
SEQ = 4096
DEPTH = 4

GRID_W = 64
CTX_LEN = 256
N_MIXERS = 2
N_MOD = 6
CHUNK_A = 128
A_WIDTH = D_MODEL
A_GROUPS = 8
A_GROUP_DIM = A_WIDTH // A_GROUPS
RET_HEADS = 4
RET_DK = D_MODEL // RET_HEADS
RET_DV = D_MODEL // RET_HEADS
RET_CHUNK = 128
ROPE_BASE = 10000.0
N_EXPERTS = 16
EC_CAPACITY = 2
EXPERT_FF = D_MODEL
EPS = 1e-6

kernel_name = 'hybrid_gmlp_retention_ec_dit'


def rms_norm(x, g):
    x32 = x.astype(jnp.float32)
    y = x32 * lax.rsqrt(jnp.mean(x32 * x32, axis=-1, keepdims=True) + EPS)
    return (y * g.astype(jnp.float32)).astype(x.dtype)


def layer_norm(x, g, b):
    x32 = x.astype(jnp.float32)
    mu = jnp.mean(x32, axis=-1, keepdims=True)
    var = jnp.mean(jnp.square(x32 - mu), axis=-1, keepdims=True)
    y = (x32 - mu) * lax.rsqrt(var + EPS)
    return (y * g.astype(jnp.float32) + b.astype(jnp.float32)).astype(x.dtype)


def head_norm(o):
    mu = jnp.mean(o, axis=-1, keepdims=True)
    var = jnp.mean(jnp.square(o - mu), axis=-1, keepdims=True)
    return (o - mu) * lax.rsqrt(var + EPS)


def modulate(h, shift, scale):
    return h * (1.0 + scale) + shift


def chunk_gmlp(h, w_in, ln_g, ln_b, w_s, b_s, w_out):
    bsz, n, _ = h.shape
    z = jax.nn.gelu(h @ w_in)
    u, v = jnp.split(z, 2, axis=-1)
    v = layer_norm(v, ln_g, ln_b)
    v = v.reshape(bsz, n // CHUNK_A, CHUNK_A, A_GROUPS, A_GROUP_DIM)
    s = jnp.einsum('gnm,bcmgd->bcngd', w_s, v) + jnp.swapaxes(b_s, 0, 1)[None, None, :, :, None].astype(v.dtype)
    return (u * s.reshape(bsz, n, A_WIDTH)) @ w_out


def split_heads(t, d):
    bsz, n, _ = t.shape
    return t.reshape(bsz, n, RET_HEADS, d).transpose(0, 2, 1, 3)


def axial_rope(t):
    n = t.shape[2]
    rows = n // GRID_W
    pos_r = jnp.repeat(jnp.arange(rows), GRID_W).astype(jnp.float32)
    pos_c = jnp.tile(jnp.arange(GRID_W), rows).astype(jnp.float32)
    n_freq = RET_DK // 4
    inv = jnp.power(ROPE_BASE, -jnp.arange(n_freq, dtype=jnp.float32) / n_freq)
    ang = jnp.concatenate([pos_r[:, None] * inv[None], pos_c[:, None] * inv[None]], axis=-1)
    cos, sin = jnp.cos(ang), jnp.sin(ang)
    t1, t2 = jnp.split(t, 2, axis=-1)
    return jnp.concatenate([t1 * cos - t2 * sin, t1 * sin + t2 * cos], axis=-1)


def decay_tables(log_g, strict):
    idx = jnp.arange(RET_CHUNK, dtype=jnp.float32)
    diff = idx[:, None] - idx[None, :]
    mask = (diff > 0) if strict else (diff >= 0)
    intra = jnp.where(mask, jnp.exp(jnp.where(mask, diff, 0.0)[None] * log_g[:, None, None]), 0.0)
    q_dec = jnp.exp((idx + 1.0)[None] * log_g[:, None])
    k_dec = jnp.exp((RET_CHUNK - 1.0 - idx)[None] * log_g[:, None])
    chunk_dec = jnp.exp(RET_CHUNK * log_g)
    return intra, q_dec, k_dec, chunk_dec


def retention_scan(q, k, v, log_g, s0, strict):
    bsz, h, n, _ = q.shape
    nc = n // RET_CHUNK
    intra, q_dec, k_dec, chunk_dec = decay_tables(log_g, strict)

    def to_chunks(t):
        return jnp.moveaxis(t.reshape(bsz, h, nc, RET_CHUNK, t.shape[-1]), 2, 0)

    def step(s, qkv):
        qc, kc, vc = qkv
        att = jnp.einsum('bhnk,bhmk->bhnm', qc, kc) * intra[None]
        o = jnp.einsum('bhnm,bhmv->bhnv', att, vc) + jnp.einsum('bhnk,bhkv->bhnv', qc * q_dec[None, :, :, None], s)
        s = chunk_dec[None, :, None, None] * s + jnp.einsum('bhmk,bhmv->bhkv', kc * k_dec[None, :, :, None], vc)
        return s, o

    s_fin, o = lax.scan(step, s0, (to_chunks(q), to_chunks(k), to_chunks(v)))
    return jnp.moveaxis(o, 0, 2).reshape(bsz, h, n, -1), s_fin


def retention_final_state(k, v, log_g):
    n = k.shape[2]
    w = jnp.exp((n - 1.0 - jnp.arange(n, dtype=jnp.float32))[None] * log_g[:, None])
    return jnp.einsum('bhnk,hn,bhnv->bhkv', k, w, v)


def retention_merge(o_f, o_b, g_f, g_b, w_out, dtype):
    def heads_out(o):
        bsz, h, n, dv = o.shape
        return head_norm(o).transpose(0, 2, 1, 3).reshape(bsz, n, h * dv)
    y = jax.nn.silu(g_f) * heads_out(o_f) + jax.nn.silu(g_b) * heads_out(o_b)
    return y.astype(dtype) @ w_out


def retention_mixer(h_ctx, h_lat, w_in, decay_f, decay_b, w_out, need_ctx_out):
    qk_w = RET_HEADS * RET_DK
    v_w = RET_HEADS * RET_DV
    k_scale = RET_DK ** -0.5
    lg_f = jax.nn.log_sigmoid(decay_f.astype(jnp.float32))
    lg_b = jax.nn.log_sigmoid(decay_b.astype(jnp.float32))

    def flip(t):
        return t[:, :, ::-1]

    def project(h):
        z = (h @ w_in).astype(jnp.float32)
        q, k, v, g_f, g_b = jnp.split(z, [qk_w, 2 * qk_w, 2 * qk_w + v_w, 2 * qk_w + 2 * v_w], axis=-1)
        return split_heads(q, RET_DK), split_heads(k, RET_DK) * k_scale, split_heads(v, RET_DV), g_f, g_b

    bsz = h_ctx.shape[0]
    if need_ctx_out:
        qc, kc, vc, gfc, gbc = project(h_ctx)
        s0 = jnp.zeros((bsz, RET_HEADS, RET_DK, RET_DV), jnp.float32)
        oc_f, s_f = retention_scan(qc, kc, vc, lg_f, s0, False)
        oc_b, s_b = retention_scan(flip(qc), flip(kc), flip(vc), lg_b, s0, True)
        y_ctx = retention_merge(oc_f, flip(oc_b), gfc, gbc, w_out, h_ctx.dtype)
    else:
        z = (h_ctx @ w_in[:, qk_w:2 * qk_w + v_w]).astype(jnp.float32)
        kc = split_heads(z[..., :qk_w], RET_DK) * k_scale
        vc = split_heads(z[..., qk_w:], RET_DV)
        s_f = retention_final_state(kc, vc, lg_f)
        s_b = retention_final_state(flip(kc), flip(vc), lg_b)
        y_ctx = None

    q, k, v, g_f, g_b = project(h_lat)
    q = axial_rope(q)
    k = axial_rope(k)
    o_f, _ = retention_scan(q, k, v, lg_f, s_f, False)
    o_b, _ = retention_scan(flip(q), flip(k), flip(v), lg_b, s_b, True)
    y_lat = retention_merge(o_f, flip(o_b), g_f, g_b, w_out, h_lat.dtype)
    return y_ctx, y_lat


def expert_choice_ffn(h, w_router, w_gate, w_up, w_down):
    bsz, n, d = h.shape
    cap = EC_CAPACITY * n // N_EXPERTS
    aff = jax.nn.softmax((h @ w_router).astype(jnp.float32), axis=-1)
    gate, idx = lax.top_k(jnp.swapaxes(aff, 1, 2), cap)
    xs = jax.vmap(lambda hb, ib: hb[ib])(h, idx)
    hid = jax.nn.silu(jnp.einsum('becd,edf->becf', xs, w_gate)) * jnp.einsum('becd,edf->becf', xs, w_up)
    ye = jnp.einsum('becf,efd->becd', hid, w_down) * gate[..., None].astype(h.dtype)
    return jax.vmap(lambda yb, ib: jnp.zeros((n, d), yb.dtype).at[ib.reshape(-1)].add(yb.reshape(-1, d)))(ye, idx)


def setup_inputs(seed: int = 0) -> dict:
    key = jax.random.key(seed)
    ks = jax.random.split(key, 24)
    n_a = (DEPTH + 1) // 2
    n_b = DEPTH // 2
    f32 = jnp.float32

    def nrm(k, shape, scale):
        return jax.random.normal(k, shape, f32) * scale

    d = D_MODEL
    decay_base = jnp.log(jnp.power(2.0, 5.0 + jnp.arange(RET_HEADS, dtype=f32)) - 1.0)
    return {
        'x': nrm(ks[0], (BATCH, SEQ, d), 1.0),
        'c': nrm(ks[1], (BATCH, d), 1.0),
        'ctx': nrm(ks[2], (BATCH, CTX_LEN, d), 1.0),
        'c_ctx': nrm(ks[3], (d,), 1.0),
        'ada_w': nrm(ks[4], (DEPTH, d, N_MOD * d), 0.5 * d ** -0.5),
        'ada_b': nrm(ks[5], (DEPTH, N_MOD * d), 0.02),
        'norm_mix_g': 1.0 + nrm(ks[6], (DEPTH, d), 0.02),
        'norm_ffn_g': 1.0 + nrm(ks[7], (DEPTH, d), 0.02),
        'a_w_in': nrm(ks[8], (n_a, d, 2 * A_WIDTH), d ** -0.5),
        'a_ln_g': 1.0 + nrm(ks[9], (n_a, A_WIDTH), 0.02),
        'a_ln_b': nrm(ks[10], (n_a, A_WIDTH), 0.02),
        'a_w_s': nrm(ks[11], (n_a, A_GROUPS, CHUNK_A, CHUNK_A), CHUNK_A ** -0.5),
        'a_b_s': 1.0 + nrm(ks[12], (n_a, A_GROUPS, CHUNK_A), 0.02),
        'a_w_out': nrm(ks[13], (n_a, A_WIDTH, d), A_WIDTH ** -0.5),
        'r_w_in': nrm(ks[14], (n_b, d, 2 * RET_HEADS * RET_DK + 3 * RET_HEADS * RET_DV), d ** -0.5),
        'r_decay_f': decay_base[None] + nrm(ks[15], (n_b, RET_HEADS), 0.1),
        'r_decay_b': decay_base[None] + nrm(ks[16], (n_b, RET_HEADS), 0.1),
        'r_w_out': nrm(ks[17], (n_b, RET_HEADS * RET_DV, d), (RET_HEADS * RET_DV) ** -0.5),
        'moe_w_router': nrm(ks[18], (DEPTH, d, N_EXPERTS), d ** -0.5),
        'moe_w_gate': nrm(ks[19], (DEPTH, N_EXPERTS, d, EXPERT_FF), d ** -0.5),
        'moe_w_up': nrm(ks[20], (DEPTH, N_EXPERTS, d, EXPERT_FF), d ** -0.5),
        'moe_w_down': nrm(ks[21], (DEPTH, N_EXPERTS, EXPERT_FF, d), EXPERT_FF ** -0.5),
        'final_norm_g': 1.0 + nrm(ks[22], (d,), 0.02),
    }


def reference(x, c, ctx, c_ctx, ada_w, ada_b, norm_mix_g, norm_ffn_g, a_w_in, a_ln_g, a_ln_b, a_w_s, a_b_s,
              a_w_out, r_w_in, r_decay_f, r_decay_b, r_w_out, moe_w_router, moe_w_gate, moe_w_up, moe_w_down,
              final_norm_g):
    d = D_MODEL
    for i in range(DEPTH):
        last = i == DEPTH - 1
        kind = i % N_MIXERS
        j = i // N_MIXERS
        mod_lat = jax.nn.silu(c) @ ada_w[i] + ada_b[i]
        sh1, sc1, g1, sh2, sc2, g2 = [m[:, None, :] for m in jnp.split(mod_lat, N_MOD, axis=-1)]
        h_lat = modulate(rms_norm(x, norm_mix_g[i]), sh1, sc1)
        ctx_read = (not last) or kind == 1
        if ctx_read:
            n_mod_ctx = N_MOD if not last else 2
            mod_ctx = jax.nn.silu(c_ctx) @ ada_w[i, :, :n_mod_ctx * d] + ada_b[i, :n_mod_ctx * d]
            parts = jnp.split(mod_ctx, n_mod_ctx)
            h_ctx = modulate(rms_norm(ctx, norm_mix_g[i]), parts[0], parts[1])
        if kind == 0:
            y_lat = chunk_gmlp(h_lat, a_w_in[j], a_ln_g[j], a_ln_b[j], a_w_s[j], a_b_s[j], a_w_out[j])
            y_ctx = None if last else chunk_gmlp(h_ctx, a_w_in[j], a_ln_g[j], a_ln_b[j], a_w_s[j], a_b_s[j], a_w_out[j])
        else:
            y_ctx, y_lat = retention_mixer(h_ctx, h_lat, r_w_in[j], r_decay_f[j], r_decay_b[j], r_w_out[j],
                                           not last)
        x = x + g1 * y_lat
        x = x + g2 * expert_choice_ffn(modulate(rms_norm(x, norm_ffn_g[i]), sh2, sc2),
                                       moe_w_router[i], moe_w_gate[i], moe_w_up[i], moe_w_down[i])
        if not last:
            ctx = ctx + parts[2] * y_ctx
            ctx = ctx + parts[5] * expert_choice_ffn(modulate(rms_norm(ctx, norm_ffn_g[i]), parts[3], parts[4]),
                                                     moe_w_router[i], moe_w_gate[i], moe_w_up[i], moe_w_down[i])
    return rms_norm(x, final_norm_g)
```

```python
import functools

import jax
import jax.numpy as jnp
from jax import lax
from jax.experimental import pallas as pl
from jax.experimental.pallas import tpu as pltpu

D = 1024
BATCH = 8
SEQ = 4096
DEPTH = 4
GRID_W = 64
CTX_LEN = 256
N_MOD = 6
CHUNK = 128
A_GROUPS = 8
HEADS = 4
DK = 256
DV = 256
ROPE_BASE = 10000.0
N_EXPERTS = 16
EC_CAPACITY = 2
EPS = 1e-6

N_LAT = BATCH * SEQ
N_CTX = BATCH * CTX_LEN
TOK = N_LAT + N_CTX
TM = 256
LAT_TILES = N_LAT // TM
TILES = TOK // TM
TILES_PER_SAMPLE = SEQ // TM
CTX_GROUP = BATCH
MOD_ROWS = 16

F32 = jnp.float32
BF16 = jnp.bfloat16
VMEM_LIMIT = 48 * 1024 * 1024


def _cparams(sem):
    return pltpu.CompilerParams(dimension_semantics=sem, vmem_limit_bytes=VMEM_LIMIT)


def _rms(x, g):
    return x * lax.rsqrt(jnp.mean(x * x, axis=-1, keepdims=True) + EPS) * g


def _silu(x):
    return x * (1.0 / (1.0 + jnp.exp(-x)))


def _gelu_tanh(x):
    return 0.5 * x * (1.0 + jnp.tanh(0.7978845608028654 * (x + 0.044715 * (x * x * x))))


def _post(x, y, mod, gffn, wrt):
    g1 = mod[:, 2 * D:3 * D]
    sh2 = mod[:, 3 * D:4 * D]
    sc2 = mod[:, 4 * D:5 * D]
    x1 = x + g1 * y
    h2 = (_rms(x1, gffn) * (1.0 + sc2) + sh2).astype(BF16)
    logits = lax.dot_general(wrt, h2, (((1,), (1,)), ((), ())), preferred_element_type=F32)
    m = jnp.max(logits, axis=0, keepdims=True)
    e = jnp.exp(logits - m)
    aff = e / jnp.sum(e, axis=0, keepdims=True)
    return x1, h2, aff


def _mods_kernel(cc_ref, w_ref, b_ref, o_ref):
    a = _silu(cc_ref[...]).astype(BF16)
    o_ref[...] = jnp.dot(a, w_ref[...].astype(BF16), preferred_element_type=F32) + b_ref[...]


def _mods(cc, ada_w, ada_b):
    tn = 1536
    return pl.pallas_call(
        _mods_kernel,
        grid=(DEPTH, N_MOD * D // tn),
        in_specs=[
            pl.BlockSpec((MOD_ROWS, D), lambda i, n: (0, 0)),
            pl.BlockSpec((None, D, tn), lambda i, n: (i, 0, n)),
            pl.BlockSpec((None, 1, tn), lambda i, n: (i, 0, n)),
        ],
        out_specs=pl.BlockSpec((None, MOD_ROWS, tn), lambda i, n: (i, 0, n)),
        out_shape=jax.ShapeDtypeStruct((DEPTH, MOD_ROWS, N_MOD * D), F32),
        compiler_params=_cparams(("parallel", "parallel")),
        name="adaln_mods",
    )(cc, ada_w, ada_b.reshape(DEPTH, 1, N_MOD * D))


def _tile_group(i):
    return jnp.minimum(i // TILES_PER_SAMPLE, CTX_GROUP)


def _gmlp_kernel(x_ref, mod_ref, gmix_ref, win_ref, lng_ref, lnb_ref, ws_ref, bs_ref, wout_ref,
                 gffn_ref, wrt_ref, x1_ref, h2_ref, aff_ref):
    x = x_ref[...]
    mod = mod_ref[...]
    h = (_rms(x, gmix_ref[...]) * (1.0 + mod[:, D:2 * D]) + mod[:, 0:D]).astype(BF16)
    z = _gelu_tanh(jnp.dot(h, win_ref[...], preferred_element_type=F32))
    u = z[:, :D]
    v = z[:, D:]
    mu = jnp.mean(v, axis=-1, keepdims=True)
    vc = v - mu
    var = jnp.mean(vc * vc, axis=-1, keepdims=True)
    v = (vc * lax.rsqrt(var + EPS) * lng_ref[...] + lnb_ref[...]).astype(BF16)
    bs = bs_ref[...]
    rows = []
    for c in range(TM // CHUNK):
        cols = []
        for g in range(A_GROUPS):
            vg = v[c * CHUNK:(c + 1) * CHUNK, g * CHUNK:(g + 1) * CHUNK]
            s = jnp.dot(ws_ref[g], vg, preferred_element_type=F32)
            cols.append(s + bs[:, g * CHUNK:(g + 1) * CHUNK])
        rows.append(jnp.concatenate(cols, axis=1))
    s = jnp.concatenate(rows, axis=0)
    y = jnp.dot((u * s).astype(BF16), wout_ref[...], preferred_element_type=F32)
    x1, h2, aff = _post(x, y, mod, gffn_ref[...], wrt_ref[...])
    x1_ref[...] = x1
    h2_ref[...] = h2
    aff_ref[...] = aff


def _gmlp_layer(x, mods_l, gmix, win, lng, lnb, ws, bs_full, wout, gffn, wrt):
    full = lambda shape: pl.BlockSpec(shape, lambda i: (0,) * len(shape))
    return pl.pallas_call(
        _gmlp_kernel,
        grid=(TILES,),
        in_specs=[
            pl.BlockSpec((TM, D), lambda i: (i, 0)),
            pl.BlockSpec((None, 1, N_MOD * D), lambda i: (_tile_group(i), 0, 0)),
            full((1, D)), full((D, 2 * D)), full((1, D)), full((1, D)),
            full((A_GROUPS, CHUNK, CHUNK)), full((CHUNK, D)), full((D, D)),
            full((1, D)), full((N_EXPERTS, D)),
        ],
        out_specs=[
            pl.BlockSpec((TM, D), lambda i: (i, 0)),
            pl.BlockSpec((TM, D), lambda i: (i, 0)),
            pl.BlockSpec((N_EXPERTS, TM), lambda i: (0, i)),
        ],
        out_shape=[
            jax.ShapeDtypeStruct((TOK, D), F32),
            jax.ShapeDtypeStruct((TOK, D), BF16),
            jax.ShapeDtypeStruct((N_EXPERTS, TOK), F32),
        ],
        compiler_params=_cparams(("parallel",)),
        name="gmlp_mixer",
    )(x, mods_l, gmix, win, lng, lnb, ws, bs_full, wout, gffn, wrt)


def _ret_proj_kernel(x_ref, mod_ref, gmix_ref, win_ref, cos_ref, sin_ref, qkv_ref, gate_ref):
    x = x_ref[...]
    mod = mod_ref[...]
    h = (_rms(x, gmix_ref[...]) * (1.0 + mod[:, D:2 * D]) + mod[:, 0:D]).astype(BF16)
    z = jnp.dot(h, win_ref[...], preferred_element_type=F32)
    cos = cos_ref[...]
    sin = sin_ref[...]
    half = DK // 2
    for part, scale in ((0, 1.0), (1, DK ** -0.5)):
        for hd in range(HEADS):
            base = part * D + hd * DK
            t1 = z[:, base:base + half]
            t2 = z[:, base + half:base + DK]
            qkv_ref[:, base:base + half] = ((t1 * cos - t2 * sin) * scale).astype(BF16)
            qkv_ref[:, base + half:base + DK] = ((t1 * sin + t2 * cos) * scale).astype(BF16)
    qkv_ref[:, 2 * D:3 * D] = z[:, 2 * D:3 * D].astype(BF16)
    gate_ref[...] = _silu(z[:, 3 * D:5 * D]).astype(BF16)


def _ret_proj(x, mods_l, gmix, win, cos_t, sin_t):
    full = lambda shape: pl.BlockSpec(shape, lambda i: (0,) * len(shape))
    rope_blk = lambda i: (jnp.where(i < LAT_TILES, i % TILES_PER_SAMPLE, TILES_PER_SAMPLE), 0)
    return pl.pallas_call(
        _ret_proj_kernel,
        grid=(TILES,),
        in_specs=[
            pl.BlockSpec((TM, D), lambda i: (i, 0)),
            pl.BlockSpec((None, 1, N_MOD * D), lambda i: (_tile_group(i), 0, 0)),
            full((1, D)), full((D, 5 * D)),
            pl.BlockSpec((TM, DK // 2), rope_blk),
            pl.BlockSpec((TM, DK // 2), rope_blk),
        ],
        out_specs=[
            pl.BlockSpec((TM, 3 * D), lambda i: (i, 0)),
            pl.BlockSpec((TM, 2 * D), lambda i: (i, 0)),
        ],
        out_shape=[
            jax.ShapeDtypeStruct((TOK, 3 * D), BF16),
            jax.ShapeDtypeStruct((TOK, 2 * D), BF16),
        ],
        compiler_params=_cparams(("parallel",)),
        name="retention_proj",
    )(x, mods_l, gmix, win, cos_t, sin_t)


SCAN_STEPS = TILES_PER_SAMPLE + 1


def _ret_scan_kernel(cdec_ref, qkv_ref, gate_ref, x_ref, mod_ref, intra_ref, qdec_ref, kdec_ref,
                     wout_ref, gffn_ref, wrt_ref, x1_ref, h2_ref, aff_ref, s_ref, yf_ref):
    p = pl.program_id(1)
    j = pl.program_id(2)

    @pl.when(j == 0)
    def _():
        s_ref[...] = jnp.zeros_like(s_ref)

    def chunk_out(c):
        r0 = c * CHUNK
        outs = []
        for hd in range(HEADS):
            q = qkv_ref[r0:r0 + CHUNK, hd * DK:(hd + 1) * DK]
            k = qkv_ref[r0:r0 + CHUNK, D + hd * DK:D + (hd + 1) * DK]
            v = qkv_ref[r0:r0 + CHUNK, 2 * D + hd * DV:2 * D + (hd + 1) * DV]
            att = lax.dot_general(q, k, (((1,), (1,)), ((), ())), preferred_element_type=F32)
            att = (att * intra_ref[hd]).astype(BF16)
            s_old = s_ref[hd]
            o = jnp.dot(att, v, preferred_element_type=F32)
            o = o + jnp.dot(q, s_old.astype(BF16), preferred_element_type=F32) * qdec_ref[hd]
            ks = (k.astype(F32) * kdec_ref[hd]).astype(BF16)
            kv = lax.dot_general(ks, v, (((0,), (0,)), ((), ())), preferred_element_type=F32)
            s_ref[hd] = cdec_ref[p, hd] * s_old + kv
            mu = jnp.mean(o, axis=-1, keepdims=True)
            oc = o - mu
            var = jnp.mean(oc * oc, axis=-1, keepdims=True)
            gate = gate_ref[r0:r0 + CHUNK, hd * DV:(hd + 1) * DV].astype(F32)
            outs.append(gate * (oc * lax.rsqrt(var + EPS)))
        return jnp.concatenate(outs, axis=1)

    @pl.when(p == 0)
    def _():
        row = pl.multiple_of(j * TM, TM)
        for c in range(TM // CHUNK):
            yf_ref[pl.ds(row + c * CHUNK, CHUNK), :] = chunk_out(c)

    @pl.when(p == 1)
    def _():
        pos = jnp.where(j == 0, 0, SCAN_STEPS - j)
        row = pl.multiple_of(pos * TM, TM)
        ys = [None] * (TM // CHUNK)
        for c in reversed(range(TM // CHUNK)):
            ys[c] = chunk_out(c) + yf_ref[pl.ds(row + c * CHUNK, CHUNK), :]
        y = jnp.concatenate(ys, axis=0).astype(BF16)
        y = jnp.dot(y, wout_ref[...], preferred_element_type=F32)
        x1, h2, aff = _post(x_ref[...], y, mod_ref[...], gffn_ref[...], wrt_ref[...])
        x1_ref[...] = x1
        h2_ref[...] = h2
        aff_ref[...] = aff


def _scan_tile(b, p, j):
    lat = b * TILES_PER_SAMPLE + jnp.where(p == 0, j - 1, TILES_PER_SAMPLE - j)
    return jnp.where(j == 0, LAT_TILES + b, lat)


def _ret_scan(qkv, gates, x, mods_l, cdec, intra, qdec, kdec, wout, gffn, wrt):
    full = lambda shape: pl.BlockSpec(shape, lambda b, p, j, *_: (0,) * len(shape))
    tile = lambda b, p, j, *_: (_scan_tile(b, p, j), 0)
    tile_p1 = lambda b, p, j, *_: (_scan_tile(b, 1, jnp.where(p == 0, 0, j)), 0)
    tile_p1_t = lambda b, p, j, *_: (0, _scan_tile(b, 1, jnp.where(p == 0, 0, j)))
    per_dir = lambda shape: pl.BlockSpec((None,) + shape, lambda b, p, j, *_: (p,) + (0,) * len(shape))
    grid_spec = pltpu.PrefetchScalarGridSpec(
        num_scalar_prefetch=0,
        grid=(BATCH, 2, SCAN_STEPS),
        in_specs=[
            pl.BlockSpec(memory_space=pltpu.SMEM),
            pl.BlockSpec((TM, 3 * D), tile),
            pl.BlockSpec((TM, D), lambda b, p, j, *_: (_scan_tile(b, p, j), p)),
            pl.BlockSpec((TM, D), tile_p1),
            pl.BlockSpec((None, 1, N_MOD * D), lambda b, p, j, *_: (jnp.where(j == 0, CTX_GROUP, b), 0, 0)),
            per_dir((HEADS, CHUNK, CHUNK)), per_dir((HEADS, CHUNK, DV)), per_dir((HEADS, CHUNK, DK)),
            full((D, D)), full((1, D)), full((N_EXPERTS, D)),
        ],
        out_specs=[
            pl.BlockSpec((TM, D), tile_p1),
            pl.BlockSpec((TM, D), tile_p1),
            pl.BlockSpec((N_EXPERTS, TM), tile_p1_t),
        ],
        scratch_shapes=[
            pltpu.VMEM((HEADS, DK, DV), F32),
            pltpu.VMEM((SCAN_STEPS * TM, D), F32),
        ],
    )
    return pl.pallas_call(
        _ret_scan_kernel,
        grid_spec=grid_spec,
        out_shape=[
            jax.ShapeDtypeStruct((TOK, D), F32),
            jax.ShapeDtypeStruct((TOK, D), BF16),
            jax.ShapeDtypeStruct((N_EXPERTS, TOK), F32),
        ],
        compiler_params=_cparams(("arbitrary", "arbitrary", "arbitrary")),
        name="retention_scan",
    )(cdec, qkv, gates, x, mods_l, intra, qdec, kdec, wout, gffn, wrt)


def _decay_tables(decay_f, decay_b):
    idx = jnp.arange(CHUNK, dtype=F32)
    diff = idx[:, None] - idx[None, :]
    lg_f = jax.nn.log_sigmoid(decay_f.astype(F32))
    lg_b = jax.nn.log_sigmoid(decay_b.astype(F32))
    mask_f = diff >= 0
    intra_f = jnp.where(mask_f, jnp.exp(jnp.where(mask_f, diff, 0.0)[None] * lg_f[:, None, None]), 0.0)
    mask_b = diff < 0
    intra_b = jnp.where(mask_b, jnp.exp(jnp.where(mask_b, -diff, 0.0)[None] * lg_b[:, None, None]), 0.0)
    qdec_f = jnp.exp((idx + 1.0)[None] * lg_f[:, None])
    kdec_f = jnp.exp((CHUNK - 1.0 - idx)[None] * lg_f[:, None])
    qdec_b = jnp.exp((CHUNK - idx)[None] * lg_b[:, None])
    kdec_b = jnp.exp(idx[None] * lg_b[:, None])
    wide = lambda t, w: jnp.broadcast_to(t[:, :, None], (HEADS, CHUNK, w))
    intra = jnp.stack([intra_f, intra_b])
    qdec = jnp.stack([wide(qdec_f, DV), wide(qdec_b, DV)])
    kdec = jnp.stack([wide(kdec_f, DK), wide(kdec_b, DK)])
    cdec = jnp.stack([jnp.exp(CHUNK * lg_f), jnp.exp(CHUNK * lg_b)])
    return cdec, intra, qdec, kdec


def _rope_tables():
    rows = SEQ // GRID_W
    pos_r = jnp.repeat(jnp.arange(rows), GRID_W).astype(F32)
    pos_c = jnp.tile(jnp.arange(GRID_W), rows).astype(F32)
    n_freq = DK // 4
    inv = jnp.power(ROPE_BASE, -jnp.arange(n_freq, dtype=F32) / n_freq)
    ang = jnp.concatenate([pos_r[:, None] * inv[None], pos_c[:, None] * inv[None]], axis=-1)
    cos = jnp.concatenate([jnp.cos(ang), jnp.ones((TM, DK // 2), F32)], axis=0)
    sin = jnp.concatenate([jnp.sin(ang), jnp.zeros((TM, DK // 2), F32)], axis=0)
    return cos, sin


def _ffn_kernel(xs_ref, wg_ref, wu_ref, wd_ref, o_ref):
    xs = xs_ref[...]
    a = jnp.dot(xs, wg_ref[...], preferred_element_type=F32)
    u = jnp.dot(xs, wu_ref[...], preferred_element_type=F32)
    hid = (_silu(a) * u).astype(BF16)
    o_ref[...] = jnp.dot(hid, wd_ref[...], preferred_element_type=F32)


def _expert_ffn(xs, wg, wu, wd):
    m = xs.shape[1]
    tmf = min(m, 512)
    wspec = pl.BlockSpec((None, D, D), lambda e, t: (e, 0, 0))
    return pl.pallas_call(
        _ffn_kernel,
        grid=(N_EXPERTS, m // tmf),
        in_specs=[pl.BlockSpec((None, tmf, D), lambda e, t: (e, t, 0)), wspec, wspec, wspec],
        out_specs=pl.BlockSpec((None, tmf, D), lambda e, t: (e, t, 0)),
        out_shape=jax.ShapeDtypeStruct((N_EXPERTS, m, D), F32),
        compiler_params=_cparams(("parallel", "parallel")),
        name="expert_ffn",
    )(xs, wg, wu, wd)


def _moe(h2, aff_t, n, wg, wu, wd):
    cap = EC_CAPACITY * n // N_EXPERTS
    aff = aff_t.reshape(N_EXPERTS, BATCH, n).transpose(1, 0, 2)
    gate, idx = lax.top_k(aff, cap)
    flat = idx + (jnp.arange(BATCH, dtype=idx.dtype) * n)[:, None, None]
    flat = flat.transpose(1, 0, 2).reshape(N_EXPERTS, BATCH * cap)
    xs = h2[flat]
    ye = _expert_ffn(xs, wg, wu, wd)
    ye = ye * gate.transpose(1, 0, 2).reshape(N_EXPERTS, BATCH * cap)[..., None]
    return jnp.zeros((BATCH * n, D), F32).at[flat.reshape(-1)].add(ye.reshape(-1, D))


def _final_kernel(x_ref, g_ref, o_ref):
    o_ref[...] = _rms(x_ref[...], g_ref[...])


def _final_norm(x, g):
    return pl.pallas_call(
        _final_kernel,
        grid=(N_LAT // 512,),
        in_specs=[pl.BlockSpec((512, D), lambda i: (i, 0)), pl.BlockSpec((1, D), lambda i: (0, 0))],
        out_specs=pl.BlockSpec((512, D), lambda i: (i, 0)),
        out_shape=jax.ShapeDtypeStruct((N_LAT, D), F32),
        compiler_params=_cparams(("parallel",)),
        name="final_norm",
    )(x, g)


def kernel(x, c, ctx, c_ctx, ada_w, ada_b, norm_mix_g, norm_ffn_g, a_w_in, a_ln_g, a_ln_b, a_w_s, a_b_s,
           a_w_out, r_w_in, r_decay_f, r_decay_b, r_w_out, moe_w_router, moe_w_gate, moe_w_up, moe_w_down,
           final_norm_g):
    xt = jnp.concatenate([x.reshape(N_LAT, D), ctx.reshape(N_CTX, D)], axis=0)
    cc = jnp.concatenate([c, c_ctx[None], jnp.zeros((MOD_ROWS - BATCH - 1, D), F32)], axis=0)
    mods = _mods(cc, ada_w, ada_b).reshape(DEPTH, MOD_ROWS, 1, N_MOD * D)
    cos_t, sin_t = _rope_tables()
    row = lambda v: v.reshape(1, -1)

    for i in range(DEPTH):
        last = i == DEPTH - 1
        j = i // 2
        gffn = row(norm_ffn_g[i])
        wrt = moe_w_router[i].T.astype(BF16)
        if i % 2 == 0:
            bs_full = jnp.repeat(a_b_s[j].T, CHUNK, axis=1)
            x1, h2, aff_t = _gmlp_layer(
                xt, mods[i], row(norm_mix_g[i]), a_w_in[j].astype(BF16), row(a_ln_g[j]), row(a_ln_b[j]),
                a_w_s[j].astype(BF16), bs_full, a_w_out[j].astype(BF16), gffn, wrt)
        else:
            qkv, gates = _ret_proj(xt, mods[i], row(norm_mix_g[i]), r_w_in[j].astype(BF16), cos_t, sin_t)
            cdec, intra, qdec, kdec = _decay_tables(r_decay_f[j], r_decay_b[j])
            x1, h2, aff_t = _ret_scan(qkv, gates, xt, mods[i], cdec, intra, qdec, kdec,
                                      r_w_out[j].astype(BF16), gffn, wrt)
        wg = moe_w_gate[i].astype(BF16)
        wu = moe_w_up[i].astype(BF16)
        wd = moe_w_down[i].astype(BF16)
        g2 = mods[i, :, 0, 5 * D:6 * D]
        moe_lat = _moe(h2[:N_LAT], aff_t[:, :N_LAT], SEQ, wg, wu, wd)
        x_lat = x1[:N_LAT] + jnp.repeat(g2[:BATCH], SEQ, axis=0) * moe_lat
        if last:
            return _final_norm(x_lat, row(final_norm_g)).reshape(BATCH, SEQ, D)
        moe_ctx = _moe(h2[N_LAT:], aff_t[:, N_LAT:], CTX_LEN, wg, wu, wd)
        x_ctx = x1[N_LAT:] + g2[CTX_GROUP][None] * moe_ctx
        xt = jnp.concatenate([x_lat, x_ctx], axis=0)
```

```python
import functools

import jax
import jax.numpy as jnp
import numpy as np
from jax import lax
from jax.experimental import pallas as pl
from jax.experimental.pallas import tpu as pltpu

D = 1024
BATCH = 8
SEQ = 4096
DEPTH = 4
GRID_W = 64
CTX_LEN = 256
N_MOD = 6
CHUNK = 128
A_GROUPS = 8
HEADS = 4
DK = 256
DV = 256
ROPE_BASE = 10000.0
N_EXPERTS = 16
EC_CAPACITY = 2
EPS = 1e-6

N_LAT = BATCH * SEQ
N_CTX = BATCH * CTX_LEN
TOK = N_LAT + N_CTX
TM = 256
LAT_TILES = N_LAT // TM
TILES = TOK // TM
TILES_PER_SAMPLE = SEQ // TM
CTX_GROUP = BATCH
MOD_ROWS = 16

F32 = jnp.float32
BF16 = jnp.bfloat16
VMEM_LIMIT = 48 * 1024 * 1024


def _cparams(sem):
    return pltpu.CompilerParams(dimension_semantics=sem, vmem_limit_bytes=VMEM_LIMIT)


def _rms(x, g):
    return x * lax.rsqrt(jnp.mean(x * x, axis=-1, keepdims=True) + EPS) * g


def _silu(x):
    return x * (1.0 / (1.0 + jnp.exp(-x)))


def _gelu_tanh(x):
    return 0.5 * x * (1.0 + jnp.tanh(0.7978845608028654 * (x + 0.044715 * (x * x * x))))


def _post(x, y, mod, gffn, wrt):
    g1 = mod[:, 2 * D:3 * D]
    sh2 = mod[:, 3 * D:4 * D]
    sc2 = mod[:, 4 * D:5 * D]
    x1 = x + g1 * y
    h2 = (_rms(x1, gffn) * (1.0 + sc2) + sh2).astype(BF16)
    logits = lax.dot_general(wrt, h2, (((1,), (1,)), ((), ())), preferred_element_type=F32)
    m = jnp.max(logits, axis=0, keepdims=True)
    e = jnp.exp(logits - m)
    aff = e / jnp.sum(e, axis=0, keepdims=True)
    return x1, h2, aff


def _mods_kernel(cc_ref, w_ref, b_ref, o_ref):
    a = _silu(cc_ref[...]).astype(BF16)
    o_ref[...] = jnp.dot(a, w_ref[...].astype(BF16), preferred_element_type=F32) + b_ref[...]


def _mods(cc, ada_w, ada_b):
    tn = 1536
    return pl.pallas_call(
        _mods_kernel,
        grid=(DEPTH, N_MOD * D // tn),
        in_specs=[
            pl.BlockSpec((MOD_ROWS, D), lambda i, n: (0, 0)),
            pl.BlockSpec((None, D, tn), lambda i, n: (i, 0, n)),
            pl.BlockSpec((None, 1, tn), lambda i, n: (i, 0, n)),
        ],
        out_specs=pl.BlockSpec((None, MOD_ROWS, tn), lambda i, n: (i, 0, n)),
        out_shape=jax.ShapeDtypeStruct((DEPTH, MOD_ROWS, N_MOD * D), F32),
        compiler_params=_cparams(("parallel", "parallel")),
        name="adaln_mods",
    )(cc, ada_w, ada_b.reshape(DEPTH, 1, N_MOD * D))


def _tile_group(i):
    return jnp.minimum(i // TILES_PER_SAMPLE, CTX_GROUP)


def _gmlp_kernel(xl_ref, xc_ref, mod_ref, gmix_ref, win_ref, lng_ref, lnb_ref, ws_ref, bs_ref, wout_ref,
                 gffn_ref, wrt_ref, x1_ref, h2_ref, aff_ref):
    x = jnp.where(pl.program_id(0) < LAT_TILES, xl_ref[...], xc_ref[...])
    mod = mod_ref[...]
    h = (_rms(x, gmix_ref[...]) * (1.0 + mod[:, D:2 * D]) + mod[:, 0:D]).astype(BF16)
    z = _gelu_tanh(jnp.dot(h, win_ref[...], preferred_element_type=F32))
    u = z[:, :D]
    v = z[:, D:]
    mu = jnp.mean(v, axis=-1, keepdims=True)
    vc = v - mu
    var = jnp.mean(vc * vc, axis=-1, keepdims=True)
    v = (vc * lax.rsqrt(var + EPS) * lng_ref[...] + lnb_ref[...]).astype(BF16)
    bs = bs_ref[...]
    rows = []
    for c in range(TM // CHUNK):
        cols = []
        for g in range(A_GROUPS):
            vg = v[c * CHUNK:(c + 1) * CHUNK, g * CHUNK:(g + 1) * CHUNK]
            s = jnp.dot(ws_ref[g], vg, preferred_element_type=F32)
            cols.append(s + bs[:, g * CHUNK:(g + 1) * CHUNK])
        rows.append(jnp.concatenate(cols, axis=1))
    s = jnp.concatenate(rows, axis=0)
    y = jnp.dot((u * s).astype(BF16), wout_ref[...], preferred_element_type=F32)
    x1, h2, aff = _post(x, y, mod, gffn_ref[...], wrt_ref[...])
    x1_ref[...] = x1
    h2_ref[...] = h2
    aff_ref[...] = aff


def _gmlp_layer(x_lat, x_ctx, ctx_tile0, mods_l, gmix, win, lng, lnb, ws, bs_full, wout, gffn, wrt):
    full = lambda shape: pl.BlockSpec(shape, lambda i: (0,) * len(shape))
    return pl.pallas_call(
        _gmlp_kernel,
        grid=(TILES,),
        in_specs=[
            pl.BlockSpec((TM, D), lambda i: (jnp.minimum(i, LAT_TILES - 1), 0)),
            pl.BlockSpec((TM, D), lambda i: (ctx_tile0 + jnp.maximum(i - LAT_TILES, 0), 0)),
            pl.BlockSpec((None, 1, N_MOD * D), lambda i: (_tile_group(i), 0, 0)),
            full((1, D)), full((D, 2 * D)), full((1, D)), full((1, D)),
            full((A_GROUPS, CHUNK, CHUNK)), full((CHUNK, D)), full((D, D)),
            full((1, D)), full((N_EXPERTS, D)),
        ],
        out_specs=[
            pl.BlockSpec((TM, D), lambda i: (i, 0)),
            pl.BlockSpec((TM, D), lambda i: (i, 0)),
            pl.BlockSpec((N_EXPERTS, TM), lambda i: (0, i)),
        ],
        out_shape=[
            jax.ShapeDtypeStruct((TOK, D), F32),
            jax.ShapeDtypeStruct((TOK, D), BF16),
            jax.ShapeDtypeStruct((N_EXPERTS, TOK), F32),
        ],
        compiler_params=_cparams(("parallel",)),
        name="gmlp_mixer",
    )(x_lat, x_ctx, mods_l, gmix, win, lng, lnb, ws, bs_full, wout, gffn, wrt)


def _ret_proj_kernel(x_ref, mod_ref, gmix_ref, win_ref, cos_ref, sin_ref, qkv_ref, gate_ref):
    x = x_ref[...]
    mod = mod_ref[...]
    h = (_rms(x, gmix_ref[...]) * (1.0 + mod[:, D:2 * D]) + mod[:, 0:D]).astype(BF16)
    z = jnp.dot(h, win_ref[...], preferred_element_type=F32)
    cos = cos_ref[...]
    sin = sin_ref[...]
    half = DK // 2
    for part, scale in ((0, 1.0), (1, DK ** -0.5)):
        for hd in range(HEADS):
            base = part * D + hd * DK
            t1 = z[:, base:base + half]
            t2 = z[:, base + half:base + DK]
            qkv_ref[:, base:base + half] = ((t1 * cos - t2 * sin) * scale).astype(BF16)
            qkv_ref[:, base + half:base + DK] = ((t1 * sin + t2 * cos) * scale).astype(BF16)
    qkv_ref[:, 2 * D:3 * D] = z[:, 2 * D:3 * D].astype(BF16)
    gate_ref[...] = _silu(z[:, 3 * D:5 * D]).astype(BF16)


def _ret_proj(x, mods_l, gmix, win, cos_t, sin_t):
    full = lambda shape: pl.BlockSpec(shape, lambda i: (0,) * len(shape))
    rope_blk = lambda i: (jnp.where(i < LAT_TILES, i % TILES_PER_SAMPLE, TILES_PER_SAMPLE), 0)
    return pl.pallas_call(
        _ret_proj_kernel,
        grid=(TILES,),
        in_specs=[
            pl.BlockSpec((TM, D), lambda i: (i, 0)),
            pl.BlockSpec((None, 1, N_MOD * D), lambda i: (_tile_group(i), 0, 0)),
            full((1, D)), full((D, 5 * D)),
            pl.BlockSpec((TM, DK // 2), rope_blk),
            pl.BlockSpec((TM, DK // 2), rope_blk),
        ],
        out_specs=[
            pl.BlockSpec((TM, 3 * D), lambda i: (i, 0)),
            pl.BlockSpec((TM, 2 * D), lambda i: (i, 0)),
        ],
        out_shape=[
            jax.ShapeDtypeStruct((TOK, 3 * D), BF16),
            jax.ShapeDtypeStruct((TOK, 2 * D), BF16),
        ],
        compiler_params=_cparams(("parallel",)),
        name="retention_proj",
    )(x, mods_l, gmix, win, cos_t, sin_t)


SCAN_STEPS = TILES_PER_SAMPLE + 1


def _ret_scan_kernel(cdec_ref, qkv_ref, gate_ref, x_ref, mod_ref, intra_ref, qdec_ref, kdec_ref,
                     wout_ref, gffn_ref, wrt_ref, x1_ref, h2_ref, aff_ref, s_ref, yf_ref):
    p = pl.program_id(1)
    j = pl.program_id(2)

    @pl.when(j == 0)
    def _():
        s_ref[...] = jnp.zeros_like(s_ref)

    def chunk_out(c):
        r0 = c * CHUNK
        outs = []
        for hd in range(HEADS):
            q = qkv_ref[r0:r0 + CHUNK, hd * DK:(hd + 1) * DK]
            k = qkv_ref[r0:r0 + CHUNK, D + hd * DK:D + (hd + 1) * DK]
            v = qkv_ref[r0:r0 + CHUNK, 2 * D + hd * DV:2 * D + (hd + 1) * DV]
            att = lax.dot_general(q, k, (((1,), (1,)), ((), ())), preferred_element_type=F32)
            att = (att * intra_ref[hd]).astype(BF16)
            s_old = s_ref[hd]
            o = jnp.dot(att, v, preferred_element_type=F32)
            o = o + jnp.dot(q, s_old.astype(BF16), preferred_element_type=F32) * qdec_ref[hd]
            ks = (k.astype(F32) * kdec_ref[hd]).astype(BF16)
            kv = lax.dot_general(ks, v, (((0,), (0,)), ((), ())), preferred_element_type=F32)
            s_ref[hd] = cdec_ref[p, hd] * s_old + kv
            mu = jnp.mean(o, axis=-1, keepdims=True)
            oc = o - mu
            var = jnp.mean(oc * oc, axis=-1, keepdims=True)
            gate = gate_ref[r0:r0 + CHUNK, hd * DV:(hd + 1) * DV].astype(F32)
            outs.append(gate * (oc * lax.rsqrt(var + EPS)))
        return jnp.concatenate(outs, axis=1)

    @pl.when(p == 0)
    def _():
        row = pl.multiple_of(j * TM, TM)
        for c in range(TM // CHUNK):
            yf_ref[pl.ds(row + c * CHUNK, CHUNK), :] = chunk_out(c)

    @pl.when(p == 1)
    def _():
        pos = jnp.where(j == 0, 0, SCAN_STEPS - j)
        row = pl.multiple_of(pos * TM, TM)
        ys = [None] * (TM // CHUNK)
        for c in reversed(range(TM // CHUNK)):
            ys[c] = chunk_out(c) + yf_ref[pl.ds(row + c * CHUNK, CHUNK), :]
        y = jnp.concatenate(ys, axis=0).astype(BF16)
        y = jnp.dot(y, wout_ref[...], preferred_element_type=F32)
        x1, h2, aff = _post(x_ref[...], y, mod_ref[...], gffn_ref[...], wrt_ref[...])
        x1_ref[...] = x1
        h2_ref[...] = h2
        aff_ref[...] = aff


def _scan_tile(b, p, j):
    lat = b * TILES_PER_SAMPLE + jnp.where(p == 0, j - 1, TILES_PER_SAMPLE - j)
    return jnp.where(j == 0, LAT_TILES + b, lat)


def _ret_scan(qkv, gates, x, mods_l, cdec, intra, qdec, kdec, wout, gffn, wrt):
    full = lambda shape: pl.BlockSpec(shape, lambda b, p, j, *_: (0,) * len(shape))
    tile = lambda b, p, j, *_: (_scan_tile(b, p, j), 0)
    tile_p1 = lambda b, p, j, *_: (_scan_tile(b, 1, jnp.where(p == 0, 0, j)), 0)
    tile_p1_t = lambda b, p, j, *_: (0, _scan_tile(b, 1, jnp.where(p == 0, 0, j)))
    per_dir = lambda shape: pl.BlockSpec((None,) + shape, lambda b, p, j, *_: (p,) + (0,) * len(shape))
    grid_spec = pltpu.PrefetchScalarGridSpec(
        num_scalar_prefetch=0,
        grid=(BATCH, 2, SCAN_STEPS),
        in_specs=[
            pl.BlockSpec(memory_space=pltpu.SMEM),
            pl.BlockSpec((TM, 3 * D), tile),
            pl.BlockSpec((TM, D), lambda b, p, j, *_: (_scan_tile(b, p, j), p)),
            pl.BlockSpec((TM, D), tile_p1),
            pl.BlockSpec((None, 1, N_MOD * D), lambda b, p, j, *_: (jnp.where(j == 0, CTX_GROUP, b), 0, 0)),
            per_dir((HEADS, CHUNK, CHUNK)), per_dir((HEADS, CHUNK, DV)), per_dir((HEADS, CHUNK, DK)),
            full((D, D)), full((1, D)), full((N_EXPERTS, D)),
        ],
        out_specs=[
            pl.BlockSpec((TM, D), tile_p1),
            pl.BlockSpec((TM, D), tile_p1),
            pl.BlockSpec((N_EXPERTS, TM), tile_p1_t),
        ],
        scratch_shapes=[
            pltpu.VMEM((HEADS, DK, DV), F32),
            pltpu.VMEM((SCAN_STEPS * TM, D), F32),
        ],
    )
    return pl.pallas_call(
        _ret_scan_kernel,
        grid_spec=grid_spec,
        out_shape=[
            jax.ShapeDtypeStruct((TOK, D), F32),
            jax.ShapeDtypeStruct((TOK, D), BF16),
            jax.ShapeDtypeStruct((N_EXPERTS, TOK), F32),
        ],
        compiler_params=_cparams(("arbitrary", "arbitrary", "arbitrary")),
        name="retention_scan",
    )(cdec, qkv, gates, x, mods_l, intra, qdec, kdec, wout, gffn, wrt)


def _decay_tables(decay_f, decay_b):
    idx = jnp.arange(CHUNK, dtype=F32)
    diff = idx[:, None] - idx[None, :]
    lg_f = jax.nn.log_sigmoid(decay_f.astype(F32))
    lg_b = jax.nn.log_sigmoid(decay_b.astype(F32))
    mask_f = diff >= 0
    intra_f = jnp.where(mask_f, jnp.exp(jnp.where(mask_f, diff, 0.0)[None] * lg_f[:, None, None]), 0.0)
    mask_b = diff < 0
    intra_b = jnp.where(mask_b, jnp.exp(jnp.where(mask_b, -diff, 0.0)[None] * lg_b[:, None, None]), 0.0)
    qdec_f = jnp.exp((idx + 1.0)[None] * lg_f[:, None])
    kdec_f = jnp.exp((CHUNK - 1.0 - idx)[None] * lg_f[:, None])
    qdec_b = jnp.exp((CHUNK - idx)[None] * lg_b[:, None])
    kdec_b = jnp.exp(idx[None] * lg_b[:, None])
    wide = lambda t, w: jnp.broadcast_to(t[:, :, None], (HEADS, CHUNK, w))
    intra = jnp.stack([intra_f, intra_b])
    qdec = jnp.stack([wide(qdec_f, DV), wide(qdec_b, DV)])
    kdec = jnp.stack([wide(kdec_f, DK), wide(kdec_b, DK)])
    cdec = jnp.stack([jnp.exp(CHUNK * lg_f), jnp.exp(CHUNK * lg_b)])
    return cdec, intra, qdec, kdec


def _rope_tables():
    rows = SEQ // GRID_W
    pos_r = np.repeat(np.arange(rows), GRID_W).astype(np.float64)
    pos_c = np.tile(np.arange(GRID_W), rows).astype(np.float64)
    n_freq = DK // 4
    inv = np.power(ROPE_BASE, -np.arange(n_freq, dtype=np.float64) / n_freq)
    ang = np.concatenate([pos_r[:, None] * inv[None], pos_c[:, None] * inv[None]], axis=-1)
    cos = np.concatenate([np.cos(ang), np.ones((TM, DK // 2))], axis=0)
    sin = np.concatenate([np.sin(ang), np.zeros((TM, DK // 2))], axis=0)
    return jnp.asarray(cos, F32), jnp.asarray(sin, F32)


LANE = 128
KEY_ONE = 0x3F800000
BISECT_STEPS = 31
REFINE_STEPS = 12


def _route_kernel(aff_ref, w2_ref, pos_ref, offs_ref, *, n, cap):
    aff = aff_ref[...]
    count = lambda m: jnp.sum(jnp.where(m, 1.0, 0.0), axis=1, keepdims=True)

    def bisect(_, carry):
        lo, hi = carry
        mid = lo + ((hi - lo + 1) >> 1)
        ok = count(aff >= pltpu.bitcast(mid, F32)) >= cap
        return jnp.where(ok, mid, lo), jnp.where(ok, hi, mid - 1)

    lo0 = jnp.zeros((N_EXPERTS, 1), jnp.int32)
    hi0 = jnp.full((N_EXPERTS, 1), KEY_ONE, jnp.int32)
    key, _ = lax.fori_loop(0, BISECT_STEPS, bisect, (lo0, hi0))

    def refine(_, carry):
        a, b = carry
        m = a + (b - a) * 0.5
        ok = count(aff >= m) >= cap
        return jnp.where(ok, m, a), jnp.where(ok, b, m)

    thr, above = lax.fori_loop(0, REFINE_STEPS, refine,
                               (pltpu.bitcast(key, F32), pltpu.bitcast(key + 1, F32)))
    gt = jnp.where(aff >= above, 1.0, 0.0)
    eq = jnp.where(aff >= thr, 1.0, 0.0) - gt
    need = cap - jnp.sum(gt, axis=1, keepdims=True)

    w2 = w2_ref[...]
    lane = lax.broadcasted_iota(jnp.int32, (N_EXPERTS, LANE), 1)

    def prefix(mask):
        carry = jnp.zeros((N_EXPERTS, LANE), F32)
        offs = jnp.zeros((N_EXPERTS, LANE), F32)
        incs = []
        for j in range(n // LANE):
            blk = mask[:, j * LANE:(j + 1) * LANE].astype(BF16)
            r = jnp.dot(blk, w2, preferred_element_type=F32)
            incs.append(r[:, :LANE] + carry)
            offs = jnp.where(lane == j, carry, offs)
            carry = carry + r[:, LANE:]
        offs = jnp.where(lane == n // LANE, carry, offs)
        return jnp.concatenate(incs, axis=1), offs

    tie_inc, _ = prefix(eq)
    sel = gt + eq * jnp.where(tie_inc <= need, 1.0, 0.0)
    inc, offs = prefix(sel)
    pos_ref[...] = jnp.where(sel > 0.0, inc.astype(jnp.int32) - 1, -1)
    offs_ref[...] = offs.astype(jnp.int32)


def _route(aff_t, w2, n, col0):
    cap = EC_CAPACITY * n // N_EXPERTS
    return pl.pallas_call(
        functools.partial(_route_kernel, n=n, cap=cap),
        grid=(BATCH,),
        in_specs=[pl.BlockSpec((N_EXPERTS, n), lambda b: (0, b + col0 // n)),
                  pl.BlockSpec((LANE, 2 * LANE), lambda b: (0, 0))],
        out_specs=[pl.BlockSpec((N_EXPERTS, n), lambda b: (0, b)),
                   pl.BlockSpec((None, N_EXPERTS, LANE), lambda b: (b, 0, 0))],
        out_shape=[jax.ShapeDtypeStruct((N_EXPERTS, BATCH * n), jnp.int32),
                   jax.ShapeDtypeStruct((BATCH, N_EXPERTS, LANE), jnp.int32)],
        compiler_params=_cparams(("parallel",)),
        name="route_%d" % n,
    )(aff_t, w2)


GATHER_WINDOW = 1280


def _gather_ffn_kernel(offs_ref, pos_ref, h2_ref, wg_ref, wu_ref, wd_ref, ye_ref, xs_ref, *, n, cap, spb):
    e = pl.program_id(0)
    b0 = pl.program_id(1) * spb
    sblk = min(cap, LANE)
    nb = n // LANE
    wb = min(GATHER_WINDOW, n) // LANE
    for s in range(spb):
        base = ((b0 + s) * N_EXPERTS + e) * LANE
        for sb in range(cap // sblk):
            s0 = sb * sblk
            rows = slice(s * cap + s0, s * cap + s0 + sblk)
            slot = s0 + lax.broadcasted_iota(jnp.int32, (sblk, LANE), 0)

            def window(first_blk, lo_slot):
                start = jnp.minimum(first_blk, nb - wb) if wb < nb else 0
                pieces = []
                for i in range(wb):
                    row = pos_ref[e, pl.ds(s * nb + start + i, 1), :]
                    row = jnp.where(row >= lo_slot, row, -1)
                    pieces.append(jnp.where(row == slot, 1.0, 0.0).astype(BF16))
                onehot = jnp.concatenate(pieces, axis=1)
                tok0 = pl.multiple_of((s * nb + start) * LANE, LANE)
                got = jnp.dot(onehot, h2_ref[pl.ds(tok0, wb * LANE), :], preferred_element_type=F32)
                return got.astype(BF16), start + wb

            first = lax.fori_loop(0, nb, lambda j, a: a + (offs_ref[base + j + 1] <= s0).astype(jnp.int32), 0)
            got, nxt = window(first, s0)
            xs_ref[rows, :] = got
            if wb < nb:
                def more(nxt):
                    return jnp.logical_and(nxt < nb, offs_ref[base + jnp.minimum(nxt, nb)] < s0 + sblk)

                def extra(nxt):
                    got, nxt2 = window(nxt, offs_ref[base + nxt])
                    xs_ref[rows, :] = xs_ref[rows, :] + got
                    return nxt2

                lax.while_loop(more, extra, nxt)
    xs = xs_ref[...]
    a = jnp.dot(xs, wg_ref[...], preferred_element_type=F32)
    u = jnp.dot(xs, wu_ref[...], preferred_element_type=F32)
    hid = (_silu(a) * u).astype(BF16)
    ye_ref[...] = jnp.dot(hid, wd_ref[...], preferred_element_type=F32).astype(BF16)


def _gather_ffn(offs, pos, h2, row0, wg, wu, wd, n, spb):
    cap = EC_CAPACITY * n // N_EXPERTS
    wspec = pl.BlockSpec((None, D, D), lambda e, b, *_: (e, 0, 0))
    grid_spec = pltpu.PrefetchScalarGridSpec(
        num_scalar_prefetch=1,
        grid=(N_EXPERTS, BATCH // spb),
        in_specs=[
            pl.BlockSpec((N_EXPERTS, spb * n // LANE, LANE), lambda e, b, *_: (0, b, 0)),
            pl.BlockSpec((spb * n, D), lambda e, b, *_: (b + row0 // (spb * n), 0)),
            wspec, wspec, wspec,
        ],
        out_specs=pl.BlockSpec((None, spb * cap, D), lambda e, b, *_: (e, b, 0)),
        scratch_shapes=[pltpu.VMEM((spb * cap, D), BF16)],
    )
    return pl.pallas_call(
        functools.partial(_gather_ffn_kernel, n=n, cap=cap, spb=spb),
        grid_spec=grid_spec,
        out_shape=jax.ShapeDtypeStruct((N_EXPERTS, BATCH * cap, D), BF16),
        compiler_params=_cparams(("arbitrary", "arbitrary")),
        name="gather_ffn_%d" % n,
    )(offs, pos, h2, wg, wu, wd)


STOK = 128


SCATTER_COLS = 256


def _scatter_kernel(offs_ref, pos_ref, aff_ref, ye_ref, x1_ref, mod_ref, fg_ref, *rest, n, cap, final):
    out_ref, pgt_ref = rest[-2], rest[-1]
    b = pl.program_id(0)
    t = pl.program_id(1)
    tiles = n // STOK
    prow = t if tiles % 8 == 0 else b * tiles + t
    win = min(cap, 2 * STOK)

    def window_start(e):
        if win == cap:
            return 0
        o0 = offs_ref[(b * N_EXPERTS + e) * LANE + t]
        shift = STOK.bit_length() - 1
        return pl.multiple_of(jnp.minimum((o0 >> shift) << shift, cap - win), STOK)

    for e in range(N_EXPERTS):
        row = pos_ref[e, pl.ds(prow, 1), :]
        gate = aff_ref[e, pl.ds(prow, 1), :]
        slot = window_start(e) + lax.broadcasted_iota(jnp.int32, (win, STOK), 0)
        pg = jnp.where(row == slot, gate, 0.0)
        pgt_ref[e] = pg.T.astype(BF16) if win >= LANE else pg.astype(BF16)
    for c in range(D // SCATTER_COLS):
        cols = slice(c * SCATTER_COLS, (c + 1) * SCATTER_COLS)
        acc = jnp.zeros((STOK, SCATTER_COLS), F32)
        for e in range(N_EXPERTS):
            yb = ye_ref[e, pl.ds(window_start(e), win), cols]
            if win >= LANE:
                acc = acc + jnp.dot(pgt_ref[e], yb, preferred_element_type=F32)
            else:
                acc = acc + lax.dot_general(pgt_ref[e], yb, (((0,), (0,)), ((), ())),
                                            preferred_element_type=F32)
        out_ref[:, cols] = x1_ref[:, cols] + mod_ref[:, 5 * D + c * SCATTER_COLS:5 * D + (c + 1) * SCATTER_COLS] * acc
    if final:
        out_ref[...] = _rms(out_ref[...], fg_ref[...])


def _scatter(offs, pos, aff, ye, x1, mods_l, fg, n, row0, group_of, final=False, carry=None):
    cap = EC_CAPACITY * n // N_EXPERTS
    win = min(cap, 2 * STOK)
    tiles = n // STOK
    blk0 = row0 // STOK
    if tiles % 8 == 0:
        prows, pidx = tiles, lambda b: b
    else:
        prows, pidx = BATCH * tiles, lambda b: 0
    in_specs = [
        pl.BlockSpec((N_EXPERTS, prows, STOK), lambda b, t, *_: (0, pidx(b), 0)),
        pl.BlockSpec((N_EXPERTS, prows, STOK), lambda b, t, *_: (0, pidx(b) + blk0 // prows, 0)),
        pl.BlockSpec((N_EXPERTS, cap, D), lambda b, t, *_: (0, b, 0)),
        pl.BlockSpec((STOK, D), lambda b, t, *_: (blk0 + b * tiles + t, 0)),
        pl.BlockSpec((None, 1, N_MOD * D), lambda b, t, *_: (group_of(b), 0, 0)),
        pl.BlockSpec((1, D), lambda b, t, *_: (0, 0)),
    ]
    args = [offs, pos, aff, ye, x1, mods_l, fg]
    aliases = {}
    if carry is not None:
        in_specs.append(pl.BlockSpec(memory_space=pl.ANY))
        args.append(carry)
        aliases = {len(args) - 1: 0}
    out_rows = BATCH * n if final else TOK
    out_blk0 = 0 if final else blk0
    grid_spec = pltpu.PrefetchScalarGridSpec(
        num_scalar_prefetch=1,
        grid=(BATCH, tiles),
        in_specs=in_specs,
        out_specs=pl.BlockSpec((STOK, D), lambda b, t, *_: (out_blk0 + b * tiles + t, 0)),
        scratch_shapes=[pltpu.VMEM((N_EXPERTS, STOK, win) if win >= LANE else (N_EXPERTS, win, STOK), BF16)],
    )
    return pl.pallas_call(
        functools.partial(_scatter_kernel, n=n, cap=cap, final=final),
        grid_spec=grid_spec,
        out_shape=jax.ShapeDtypeStruct((out_rows, D), F32),
        input_output_aliases=aliases,
        compiler_params=_cparams(("arbitrary", "arbitrary")),
        name="scatter_%d" % n,
    )(*args)


def _prefix_weights():
    k = jnp.arange(LANE)
    upper = (k[:, None] <= k[None, :]).astype(BF16)
    return jnp.concatenate([upper, jnp.ones((LANE, LANE), BF16)], axis=1)


def kernel(x, c, ctx, c_ctx, ada_w, ada_b, norm_mix_g, norm_ffn_g, a_w_in, a_ln_g, a_ln_b, a_w_s, a_b_s,
           a_w_out, r_w_in, r_decay_f, r_decay_b, r_w_out, moe_w_router, moe_w_gate, moe_w_up, moe_w_down,
           final_norm_g):
    xt = None
    cc = jnp.concatenate([c, c_ctx[None], jnp.zeros((MOD_ROWS - BATCH - 1, D), F32)], axis=0)
    mods = _mods(cc, ada_w, ada_b).reshape(DEPTH, MOD_ROWS, 1, N_MOD * D)
    cos_t, sin_t = _rope_tables()
    w2 = _prefix_weights()
    row = lambda v: v.reshape(1, -1)
    fg = row(final_norm_g)

    for i in range(DEPTH):
        last = i == DEPTH - 1
        j = i // 2
        gffn = row(norm_ffn_g[i])
        wrt = moe_w_router[i].T.astype(BF16)
        if i % 2 == 0:
            bs_full = jnp.repeat(a_b_s[j].T, CHUNK, axis=1)
            srcs = (x.reshape(N_LAT, D), ctx.reshape(N_CTX, D), 0) if i == 0 else (xt, xt, LAT_TILES)
            x1, h2, aff_t = _gmlp_layer(
                *srcs, mods[i], row(norm_mix_g[i]), a_w_in[j].astype(BF16), row(a_ln_g[j]), row(a_ln_b[j]),
                a_w_s[j].astype(BF16), bs_full, a_w_out[j].astype(BF16), gffn, wrt)
        else:
            qkv, gates = _ret_proj(xt, mods[i], row(norm_mix_g[i]), r_w_in[j].astype(BF16), cos_t, sin_t)
            cdec, intra, qdec, kdec = _decay_tables(r_decay_f[j], r_decay_b[j])
            x1, h2, aff_t = _ret_scan(qkv, gates, xt, mods[i], cdec, intra, qdec, kdec,
                                      r_w_out[j].astype(BF16), gffn, wrt)
        wg = moe_w_gate[i].astype(BF16)
        wu = moe_w_up[i].astype(BF16)
        wd = moe_w_down[i].astype(BF16)
        aff3 = aff_t.reshape(N_EXPERTS, TOK // STOK, STOK)

        pos, offs = _route(aff_t, w2, SEQ, 0)
        offs = offs.reshape(-1)
        pos = pos.reshape(N_EXPERTS, N_LAT // LANE, LANE)
        ye = _gather_ffn(offs, pos, h2, 0, wg, wu, wd, SEQ, 1)
        xt = _scatter(offs, pos, aff3, ye, x1, mods[i], fg, SEQ, 0, lambda b: b, final=last)
        if last:
            return xt.reshape(BATCH, SEQ, D)
        pos, offs = _route(aff_t, w2, CTX_LEN, N_LAT)
        offs = offs.reshape(-1)
        pos = pos.reshape(N_EXPERTS, N_CTX // LANE, LANE)
        ye = _gather_ffn(offs, pos, h2, N_LAT, wg, wu, wd, CTX_LEN, BATCH)
        xt = _scatter(offs, pos, aff3, ye, x1, mods[i], fg, CTX_LEN, N_LAT, lambda b: CTX_GROUP, carry=xt)
```

```python
import functools

import jax
import jax.numpy as jnp
import numpy as np
from jax import lax
from jax.experimental import pallas as pl
from jax.experimental.pallas import tpu as pltpu

D = 1024
BATCH = 8
SEQ = 4096
DEPTH = 4
GRID_W = 64
CTX_LEN = 256
N_MOD = 6
CHUNK = 128
RET_CHUNK = 256
LANE = 128
A_GROUPS = 8
HEADS = 4
DK = 256
DV = 256
ROPE_BASE = 10000.0
N_EXPERTS = 16
EC_CAPACITY = 2
EPS = 1e-6

N_LAT = BATCH * SEQ
N_CTX = BATCH * CTX_LEN
TOK = N_LAT + N_CTX
TM = 256
LAT_TILES = N_LAT // TM
TILES = TOK // TM
TILES_PER_SAMPLE = SEQ // TM
CTX_GROUP = BATCH
MOD_ROWS = 16

F32 = jnp.float32
BF16 = jnp.bfloat16
VMEM_LIMIT = 48 * 1024 * 1024


def _cparams(sem):
    return pltpu.CompilerParams(dimension_semantics=sem, vmem_limit_bytes=VMEM_LIMIT)


def _rms(x, g):
    return x * lax.rsqrt(jnp.mean(x * x, axis=-1, keepdims=True) + EPS) * g


def _silu(x):
    return x * (1.0 / (1.0 + jnp.exp(-x)))


def _gelu_tanh(x):
    return 0.5 * x * (1.0 + jnp.tanh(0.7978845608028654 * (x + 0.044715 * (x * x * x))))


def _post(x, y, mod, gffn, wrt):
    g1 = mod[:, 2 * D:3 * D]
    sh2 = mod[:, 3 * D:4 * D]
    sc2 = mod[:, 4 * D:5 * D]
    x1 = x + g1 * y
    h2 = (_rms(x1, gffn) * (1.0 + sc2) + sh2).astype(BF16)
    logits = lax.dot_general(wrt, h2, (((1,), (1,)), ((), ())), preferred_element_type=F32)
    m = jnp.max(logits, axis=0, keepdims=True)
    e = jnp.exp(logits - m)
    aff = e / jnp.sum(e, axis=0, keepdims=True)
    return x1, h2, aff


def _mods_kernel(cc_ref, w_ref, b_ref, o_ref):
    a = _silu(cc_ref[...]).astype(BF16)
    o_ref[...] = jnp.dot(a, w_ref[...].astype(BF16), preferred_element_type=F32) + b_ref[...]


def _mods(cc, ada_w, ada_b):
    tn = 1536
    return pl.pallas_call(
        _mods_kernel,
        grid=(DEPTH, N_MOD * D // tn),
        in_specs=[
            pl.BlockSpec((MOD_ROWS, D), lambda i, n: (0, 0)),
            pl.BlockSpec((None, D, tn), lambda i, n: (i, 0, n)),
            pl.BlockSpec((None, 1, tn), lambda i, n: (i, 0, n)),
        ],
        out_specs=pl.BlockSpec((None, MOD_ROWS, tn), lambda i, n: (i, 0, n)),
        out_shape=jax.ShapeDtypeStruct((DEPTH, MOD_ROWS, N_MOD * D), F32),
        compiler_params=_cparams(("parallel", "parallel")),
        name="adaln_mods",
    )(cc, ada_w, ada_b.reshape(DEPTH, 1, N_MOD * D))


def _tile_group(i):
    return jnp.minimum(i // TILES_PER_SAMPLE, CTX_GROUP)


def _gmlp_kernel(xl_ref, xc_ref, mod_ref, gmix_ref, win_ref, lng_ref, lnb_ref, ws_ref, bs_ref, wout_ref,
                 gffn_ref, wrt_ref, x1_ref, h2_ref, aff_ref):
    x = jnp.where(pl.program_id(0) < LAT_TILES, xl_ref[...], xc_ref[...])
    mod = mod_ref[...]
    h = (_rms(x, gmix_ref[...]) * (1.0 + mod[:, D:2 * D]) + mod[:, 0:D]).astype(BF16)
    z = _gelu_tanh(jnp.dot(h, win_ref[...], preferred_element_type=F32))
    u = z[:, :D]
    v = z[:, D:]
    mu = jnp.mean(v, axis=-1, keepdims=True)
    vc = v - mu
    var = jnp.mean(vc * vc, axis=-1, keepdims=True)
    v = (vc * lax.rsqrt(var + EPS) * lng_ref[...] + lnb_ref[...]).astype(BF16)
    bs = bs_ref[...]
    rows = []
    for c in range(TM // CHUNK):
        cols = []
        for g in range(A_GROUPS):
            vg = v[c * CHUNK:(c + 1) * CHUNK, g * CHUNK:(g + 1) * CHUNK]
            s = jnp.dot(ws_ref[g], vg, preferred_element_type=F32)
            cols.append(s + bs[:, g * CHUNK:(g + 1) * CHUNK])
        rows.append(jnp.concatenate(cols, axis=1))
    s = jnp.concatenate(rows, axis=0)
    y = jnp.dot((u * s).astype(BF16), wout_ref[...], preferred_element_type=F32)
    x1, h2, aff = _post(x, y, mod, gffn_ref[...], wrt_ref[...])
    x1_ref[...] = x1
    h2_ref[...] = h2
    aff_ref[...] = aff


def _gmlp_layer(x_lat, x_ctx, mods_l, gmix, win, lng, lnb, ws, bs_full, wout, gffn, wrt):
    full = lambda shape: pl.BlockSpec(shape, lambda i: (0,) * len(shape))
    return pl.pallas_call(
        _gmlp_kernel,
        grid=(TILES,),
        in_specs=[
            pl.BlockSpec((TM, D), lambda i: (jnp.minimum(i, LAT_TILES - 1), 0)),
            pl.BlockSpec((TM, D), lambda i: (jnp.maximum(i - LAT_TILES, 0), 0)),
            pl.BlockSpec((None, 1, N_MOD * D), lambda i: (_tile_group(i), 0, 0)),
            full((1, D)), full((D, 2 * D)), full((1, D)), full((1, D)),
            full((A_GROUPS, CHUNK, CHUNK)), full((CHUNK, D)), full((D, D)),
            full((1, D)), full((N_EXPERTS, D)),
        ],
        out_specs=[
            pl.BlockSpec((TM, D), lambda i: (i, 0)),
            pl.BlockSpec((TM, D), lambda i: (i, 0)),
            pl.BlockSpec((N_EXPERTS, TM), lambda i: (0, i)),
        ],
        out_shape=[
            jax.ShapeDtypeStruct((TOK, D), F32),
            jax.ShapeDtypeStruct((TOK, D), BF16),
            jax.ShapeDtypeStruct((N_EXPERTS, TOK), F32),
        ],
        compiler_params=_cparams(("parallel",)),
        name="gmlp_mixer",
    )(x_lat, x_ctx, mods_l, gmix, win, lng, lnb, ws, bs_full, wout, gffn, wrt)


def _ret_proj_kernel(xl_ref, xc_ref, mod_ref, gmix_ref, win_ref, cos_ref, sin_ref, qkv_ref, gate_ref):
    x = jnp.where(pl.program_id(0) < LAT_TILES, xl_ref[...], xc_ref[...])
    mod = mod_ref[...]
    h = (_rms(x, gmix_ref[...]) * (1.0 + mod[:, D:2 * D]) + mod[:, 0:D]).astype(BF16)
    z = jnp.dot(h, win_ref[...], preferred_element_type=F32)
    cos = cos_ref[...]
    sin = sin_ref[...]
    half = DK // 2
    for part, scale in ((0, 1.0), (1, DK ** -0.5)):
        for hd in range(HEADS):
            base = part * D + hd * DK
            t1 = z[:, base:base + half]
            t2 = z[:, base + half:base + DK]
            qkv_ref[:, base:base + half] = ((t1 * cos - t2 * sin) * scale).astype(BF16)
            qkv_ref[:, base + half:base + DK] = ((t1 * sin + t2 * cos) * scale).astype(BF16)
    qkv_ref[:, 2 * D:3 * D] = z[:, 2 * D:3 * D].astype(BF16)
    gate_ref[...] = _silu(z[:, 3 * D:5 * D]).astype(BF16)


def _ret_proj(x_lat, x_ctx, mods_l, gmix, win, cos_t, sin_t):
    full = lambda shape: pl.BlockSpec(shape, lambda i: (0,) * len(shape))
    rope_blk = lambda i: (jnp.where(i < LAT_TILES, i % TILES_PER_SAMPLE, TILES_PER_SAMPLE), 0)
    return pl.pallas_call(
        _ret_proj_kernel,
        grid=(TILES,),
        in_specs=[
            pl.BlockSpec((TM, D), lambda i: (jnp.minimum(i, LAT_TILES - 1), 0)),
            pl.BlockSpec((TM, D), lambda i: (jnp.maximum(i - LAT_TILES, 0), 0)),
            pl.BlockSpec((None, 1, N_MOD * D), lambda i: (_tile_group(i), 0, 0)),
            full((1, D)), full((D, 5 * D)),
            pl.BlockSpec((TM, DK // 2), rope_blk),
            pl.BlockSpec((TM, DK // 2), rope_blk),
        ],
        out_specs=[
            pl.BlockSpec((TM, 3 * D), lambda i: (i, 0)),
            pl.BlockSpec((TM, 2 * D), lambda i: (i, 0)),
        ],
        out_shape=[
            jax.ShapeDtypeStruct((TOK, 3 * D), BF16),
            jax.ShapeDtypeStruct((TOK, 2 * D), BF16),
        ],
        compiler_params=_cparams(("parallel",)),
        name="retention_proj",
    )(x_lat, x_ctx, mods_l, gmix, win, cos_t, sin_t)


SCAN_STEPS = TILES_PER_SAMPLE + 1


def _ret_scan_kernel(cdec_ref, qkv_ref, gate_ref, xl_ref, xc_ref, mod_ref, intra_ref, qdec_ref, kdec_ref,
                     wout_ref, gffn_ref, wrt_ref, x1_ref, h2_ref, aff_ref, s_ref, yf_ref):
    p = pl.program_id(1)
    j = pl.program_id(2)

    @pl.when(j == 0)
    def _():
        s_ref[...] = jnp.zeros_like(s_ref)

    def chunk_out(c):
        r0 = c * RET_CHUNK
        outs = []
        wide = lambda t: jnp.concatenate([t] * (DK // LANE), axis=1)
        for hd in range(HEADS):
            q = qkv_ref[r0:r0 + RET_CHUNK, hd * DK:(hd + 1) * DK]
            k = qkv_ref[r0:r0 + RET_CHUNK, D + hd * DK:D + (hd + 1) * DK]
            v = qkv_ref[r0:r0 + RET_CHUNK, 2 * D + hd * DV:2 * D + (hd + 1) * DV]
            att = lax.dot_general(q, k, (((1,), (1,)), ((), ())), preferred_element_type=F32)
            att = (att * intra_ref[hd]).astype(BF16)
            s_old = s_ref[hd]
            o = jnp.dot(att, v, preferred_element_type=F32)
            o = o + jnp.dot(q, s_old.astype(BF16), preferred_element_type=F32) * wide(qdec_ref[hd])
            ks = (k.astype(F32) * wide(kdec_ref[hd])).astype(BF16)
            kv = lax.dot_general(ks, v, (((0,), (0,)), ((), ())), preferred_element_type=F32)
            s_ref[hd] = cdec_ref[p, hd] * s_old + kv
            mu = jnp.mean(o, axis=-1, keepdims=True)
            oc = o - mu
            var = jnp.mean(oc * oc, axis=-1, keepdims=True)
            gate = gate_ref[r0:r0 + RET_CHUNK, hd * DV:(hd + 1) * DV].astype(F32)
            outs.append(gate * (oc * lax.rsqrt(var + EPS)))
        return jnp.concatenate(outs, axis=1)

    @pl.when(p == 0)
    def _():
        row = pl.multiple_of(j * TM, TM)
        for c in range(TM // RET_CHUNK):
            yf_ref[pl.ds(row + c * RET_CHUNK, RET_CHUNK), :] = chunk_out(c)

    @pl.when(p == 1)
    def _():
        pos = jnp.where(j == 0, 0, SCAN_STEPS - j)
        row = pl.multiple_of(pos * TM, TM)
        ys = [None] * (TM // RET_CHUNK)
        for c in reversed(range(TM // RET_CHUNK)):
            ys[c] = chunk_out(c) + yf_ref[pl.ds(row + c * RET_CHUNK, RET_CHUNK), :]
        y = jnp.concatenate(ys, axis=0).astype(BF16)
        y = jnp.dot(y, wout_ref[...], preferred_element_type=F32)
        x = jnp.where(j == 0, xc_ref[...], xl_ref[...])
        x1, h2, aff = _post(x, y, mod_ref[...], gffn_ref[...], wrt_ref[...])
        x1_ref[...] = x1
        h2_ref[...] = h2
        aff_ref[...] = aff


def _scan_tile(b, p, j):
    lat = b * TILES_PER_SAMPLE + jnp.where(p == 0, j - 1, TILES_PER_SAMPLE - j)
    return jnp.where(j == 0, LAT_TILES + b, lat)


def _ret_scan(qkv, gates, x_lat, x_ctx, mods_l, cdec, intra, qdec, kdec, wout, gffn, wrt):
    full = lambda shape: pl.BlockSpec(shape, lambda b, p, j, *_: (0,) * len(shape))
    tile = lambda b, p, j, *_: (_scan_tile(b, p, j), 0)
    tile_p1 = lambda b, p, j, *_: (_scan_tile(b, 1, jnp.where(p == 0, 0, j)), 0)
    tile_p1_t = lambda b, p, j, *_: (0, _scan_tile(b, 1, jnp.where(p == 0, 0, j)))
    lat_p1 = lambda b, p, j, *_: (b * TILES_PER_SAMPLE + TILES_PER_SAMPLE - jnp.where(p == 0, 1, jnp.maximum(j, 1)), 0)
    per_dir = lambda shape: pl.BlockSpec((None,) + shape, lambda b, p, j, *_: (p,) + (0,) * len(shape))
    grid_spec = pltpu.PrefetchScalarGridSpec(
        num_scalar_prefetch=0,
        grid=(BATCH, 2, SCAN_STEPS),
        in_specs=[
            pl.BlockSpec(memory_space=pltpu.SMEM),
            pl.BlockSpec((TM, 3 * D), tile),
            pl.BlockSpec((TM, D), lambda b, p, j, *_: (_scan_tile(b, p, j), p)),
            pl.BlockSpec((TM, D), lat_p1),
            pl.BlockSpec((TM, D), lambda b, p, j, *_: (b, 0)),
            pl.BlockSpec((None, 1, N_MOD * D), lambda b, p, j, *_: (jnp.where(j == 0, CTX_GROUP, b), 0, 0)),
            per_dir((HEADS, RET_CHUNK, RET_CHUNK)), per_dir((HEADS, RET_CHUNK, LANE)),
            per_dir((HEADS, RET_CHUNK, LANE)),
            full((D, D)), full((1, D)), full((N_EXPERTS, D)),
        ],
        out_specs=[
            pl.BlockSpec((TM, D), tile_p1),
            pl.BlockSpec((TM, D), tile_p1),
            pl.BlockSpec((N_EXPERTS, TM), tile_p1_t),
        ],
        scratch_shapes=[
            pltpu.VMEM((HEADS, DK, DV), F32),
            pltpu.VMEM((SCAN_STEPS * TM, D), F32),
        ],
    )
    return pl.pallas_call(
        _ret_scan_kernel,
        grid_spec=grid_spec,
        out_shape=[
            jax.ShapeDtypeStruct((TOK, D), F32),
            jax.ShapeDtypeStruct((TOK, D), BF16),
            jax.ShapeDtypeStruct((N_EXPERTS, TOK), F32),
        ],
        compiler_params=_cparams(("arbitrary", "arbitrary", "arbitrary")),
        name="retention_scan",
    )(cdec, qkv, gates, x_lat, x_ctx, mods_l, intra, qdec, kdec, wout, gffn, wrt)


def _decay_tables(decay_f, decay_b):
    idx = jnp.arange(RET_CHUNK, dtype=F32)
    diff = idx[:, None] - idx[None, :]
    lg_f = jax.nn.log_sigmoid(decay_f.astype(F32))
    lg_b = jax.nn.log_sigmoid(decay_b.astype(F32))
    mask_f = diff >= 0
    intra_f = jnp.where(mask_f, jnp.exp(jnp.where(mask_f, diff, 0.0)[None] * lg_f[:, None, None]), 0.0)
    mask_b = diff < 0
    intra_b = jnp.where(mask_b, jnp.exp(jnp.where(mask_b, -diff, 0.0)[None] * lg_b[:, None, None]), 0.0)
    qdec_f = jnp.exp((idx + 1.0)[None] * lg_f[:, None])
    kdec_f = jnp.exp((RET_CHUNK - 1.0 - idx)[None] * lg_f[:, None])
    qdec_b = jnp.exp((RET_CHUNK - idx)[None] * lg_b[:, None])
    kdec_b = jnp.exp(idx[None] * lg_b[:, None])
    wide = lambda t: jnp.broadcast_to(t[:, :, None], (HEADS, RET_CHUNK, LANE))
    intra = jnp.stack([intra_f, intra_b])
    qdec = jnp.stack([wide(qdec_f), wide(qdec_b)])
    kdec = jnp.stack([wide(kdec_f), wide(kdec_b)])
    cdec = jnp.stack([jnp.exp(RET_CHUNK * lg_f), jnp.exp(RET_CHUNK * lg_b)])
    return cdec, intra, qdec, kdec


def _rope_tables():
    rows = SEQ // GRID_W
    pos_r = np.repeat(np.arange(rows), GRID_W).astype(np.float64)
    pos_c = np.tile(np.arange(GRID_W), rows).astype(np.float64)
    n_freq = DK // 4
    inv = np.power(ROPE_BASE, -np.arange(n_freq, dtype=np.float64) / n_freq)
    ang = np.concatenate([pos_r[:, None] * inv[None], pos_c[:, None] * inv[None]], axis=-1)
    cos = np.concatenate([np.cos(ang), np.ones((TM, DK // 2))], axis=0)
    sin = np.concatenate([np.sin(ang), np.zeros((TM, DK // 2))], axis=0)
    return jnp.asarray(cos, F32), jnp.asarray(sin, F32)


KEY_ONE = 0x3F800000
BISECT_STEPS = 31
REFINE_STEPS = 12


def _route_kernel(aff_ref, w2_ref, pos_ref, offs_ref, *, n, cap):
    sample = lambda ref, s: ref[:, s * n:(s + 1) * n]
    count = lambda m: jnp.sum(jnp.where(m, 1.0, 0.0), axis=1, keepdims=True)

    def bisect(_, carry):
        nxt = []
        for s in range(BATCH):
            lo, hi = carry[s]
            mid = lo + ((hi - lo + 1) >> 1)
            ok = count(sample(aff_ref, s) >= pltpu.bitcast(mid, F32)) >= cap
            nxt.append((jnp.where(ok, mid, lo), jnp.where(ok, hi, mid - 1)))
        return tuple(nxt)

    lo0 = jnp.zeros((N_EXPERTS, 1), jnp.int32)
    hi0 = jnp.full((N_EXPERTS, 1), KEY_ONE, jnp.int32)
    keys = lax.fori_loop(0, BISECT_STEPS, bisect, ((lo0, hi0),) * BATCH)

    def refine(_, carry):
        nxt = []
        for s in range(BATCH):
            a, b = carry[s]
            m = a + (b - a) * 0.5
            ok = count(sample(aff_ref, s) >= m) >= cap
            nxt.append((jnp.where(ok, m, a), jnp.where(ok, b, m)))
        return tuple(nxt)

    cuts = lax.fori_loop(0, REFINE_STEPS, refine,
                         tuple((pltpu.bitcast(k, F32), pltpu.bitcast(k + 1, F32)) for k, _ in keys))

    w2 = w2_ref[...]
    lane = lax.broadcasted_iota(jnp.int32, (N_EXPERTS, LANE), 1)

    def prefix(mask):
        carry = jnp.zeros((N_EXPERTS, LANE), F32)
        offs = jnp.zeros((N_EXPERTS, LANE), F32)
        incs = []
        for j in range(n // LANE):
            blk = mask[:, j * LANE:(j + 1) * LANE].astype(BF16)
            r = jnp.dot(blk, w2, preferred_element_type=F32)
            incs.append(r[:, :LANE] + carry)
            offs = jnp.where(lane == j, carry, offs)
            carry = carry + r[:, LANE:]
        offs = jnp.where(lane == n // LANE, carry, offs)
        return jnp.concatenate(incs, axis=1), offs

    for s in range(BATCH):
        aff = sample(aff_ref, s)
        thr, above = cuts[s]
        gt = jnp.where(aff >= above, 1.0, 0.0)
        eq = jnp.where(aff >= thr, 1.0, 0.0) - gt
        need = cap - jnp.sum(gt, axis=1, keepdims=True)
        tie_inc, _ = prefix(eq)
        sel = gt + eq * jnp.where(tie_inc <= need, 1.0, 0.0)
        inc, offs = prefix(sel)
        pos_ref[:, s * n:(s + 1) * n] = jnp.where(sel > 0.0, inc.astype(jnp.int32) - 1, -1)
        offs_ref[s] = offs.astype(jnp.int32)


def _route(aff_t, w2, n, col0):
    cap = EC_CAPACITY * n // N_EXPERTS
    return pl.pallas_call(
        functools.partial(_route_kernel, n=n, cap=cap),
        grid=(1,),
        in_specs=[pl.BlockSpec((N_EXPERTS, BATCH * n), lambda i: (0, col0 // (BATCH * n))),
                  pl.BlockSpec((LANE, 2 * LANE), lambda i: (0, 0))],
        out_specs=[pl.BlockSpec((N_EXPERTS, BATCH * n), lambda i: (0, 0)),
                   pl.BlockSpec((BATCH, N_EXPERTS, LANE), lambda i: (0, 0, 0))],
        out_shape=[jax.ShapeDtypeStruct((N_EXPERTS, BATCH * n), jnp.int32),
                   jax.ShapeDtypeStruct((BATCH, N_EXPERTS, LANE), jnp.int32)],
        compiler_params=_cparams(("arbitrary",)),
        name="route_%d" % n,
    )(aff_t, w2)


GATHER_WINDOW = 1280


def _gather_ffn_kernel(offs_ref, pos_ref, h2_ref, wg_ref, wu_ref, wd_ref, ye_ref, xs_ref, *, n, cap, spb):
    e = pl.program_id(0)
    b0 = pl.program_id(1) * spb
    sblk = min(cap, LANE)
    nb = n // LANE
    wb = min(GATHER_WINDOW, n) // LANE
    for s in range(spb):
        base = ((b0 + s) * N_EXPERTS + e) * LANE
        for sb in range(cap // sblk):
            s0 = sb * sblk
            rows = slice(s * cap + s0, s * cap + s0 + sblk)
            slot = s0 + lax.broadcasted_iota(jnp.int32, (sblk, LANE), 0)

            def window(first_blk, lo_slot):
                start = jnp.minimum(first_blk, nb - wb) if wb < nb else 0
                pieces = []
                for i in range(wb):
                    row = pos_ref[e, pl.ds(s * nb + start + i, 1), :]
                    row = jnp.where(row >= lo_slot, row, -1)
                    pieces.append(jnp.where(row == slot, 1.0, 0.0).astype(BF16))
                onehot = jnp.concatenate(pieces, axis=1)
                tok0 = pl.multiple_of((s * nb + start) * LANE, LANE)
                got = jnp.dot(onehot, h2_ref[pl.ds(tok0, wb * LANE), :], preferred_element_type=F32)
                return got.astype(BF16), start + wb

            first = lax.fori_loop(0, nb, lambda j, a: a + (offs_ref[base + j + 1] <= s0).astype(jnp.int32), 0)
            got, nxt = window(first, s0)
            xs_ref[rows, :] = got
            if wb < nb:
                def more(nxt):
                    return jnp.logical_and(nxt < nb, offs_ref[base + jnp.minimum(nxt, nb)] < s0 + sblk)

                def extra(nxt):
                    got, nxt2 = window(nxt, offs_ref[base + nxt])
                    xs_ref[rows, :] = xs_ref[rows, :] + got
                    return nxt2

                lax.while_loop(more, extra, nxt)
    xs = xs_ref[...]
    a = jnp.dot(xs, wg_ref[...], preferred_element_type=F32)
    u = jnp.dot(xs, wu_ref[...], preferred_element_type=F32)
    hid = (_silu(a) * u).astype(BF16)
    ye_ref[...] = jnp.dot(hid, wd_ref[...], preferred_element_type=F32).astype(BF16)


def _gather_ffn(offs, pos, h2, row0, layer, wg, wu, wd, n, spb):
    cap = EC_CAPACITY * n // N_EXPERTS
    wspec = pl.BlockSpec((None, None, D, D), lambda e, b, *_: (layer, e, 0, 0))
    grid_spec = pltpu.PrefetchScalarGridSpec(
        num_scalar_prefetch=1,
        grid=(N_EXPERTS, BATCH // spb),
        in_specs=[
            pl.BlockSpec((N_EXPERTS, spb * n // LANE, LANE), lambda e, b, *_: (0, b, 0)),
            pl.BlockSpec((spb * n, D), lambda e, b, *_: (b + row0 // (spb * n), 0)),
            wspec, wspec, wspec,
        ],
        out_specs=pl.BlockSpec((None, spb * cap, D), lambda e, b, *_: (e, b, 0)),
        scratch_shapes=[pltpu.VMEM((spb * cap, D), BF16)],
    )
    return pl.pallas_call(
        functools.partial(_gather_ffn_kernel, n=n, cap=cap, spb=spb),
        grid_spec=grid_spec,
        out_shape=jax.ShapeDtypeStruct((N_EXPERTS, BATCH * cap, D), BF16),
        compiler_params=_cparams(("arbitrary", "arbitrary")),
        name="gather_ffn_%d" % n,
    )(offs, pos, h2, wg, wu, wd)


STOK = 128
SCATTER_COLS = 256


def _scatter_kernel(offs_ref, pos_ref, aff_ref, ye_ref, x1_ref, mod_ref, fg_ref, out_ref, pgt_ref,
                    *, n, cap, final):
    b = pl.program_id(0)
    t = pl.program_id(1)
    tiles = n // STOK
    prow = t if tiles % 8 == 0 else b * tiles + t
    win = min(cap, 2 * STOK)

    def window_start(e):
        if win == cap:
            return 0
        o0 = offs_ref[(b * N_EXPERTS + e) * LANE + t]
        shift = STOK.bit_length() - 1
        return pl.multiple_of(jnp.minimum((o0 >> shift) << shift, cap - win), STOK)

    for e in range(N_EXPERTS):
        row = pos_ref[e, pl.ds(prow, 1), :]
        gate = aff_ref[e, pl.ds(prow, 1), :]
        slot = window_start(e) + lax.broadcasted_iota(jnp.int32, (win, STOK), 0)
        pg = jnp.where(row == slot, gate, 0.0)
        pgt_ref[e] = pg.T.astype(BF16) if win >= LANE else pg.astype(BF16)
    for c in range(D // SCATTER_COLS):
        cols = slice(c * SCATTER_COLS, (c + 1) * SCATTER_COLS)
        acc = jnp.zeros((STOK, SCATTER_COLS), F32)
        for e in range(N_EXPERTS):
            yb = ye_ref[e, pl.ds(window_start(e), win), cols]
            if win >= LANE:
                acc = acc + jnp.dot(pgt_ref[e], yb, preferred_element_type=F32)
            else:
                acc = acc + lax.dot_general(pgt_ref[e], yb, (((0,), (0,)), ((), ())),
                                            preferred_element_type=F32)
        out_ref[:, cols] = x1_ref[:, cols] + mod_ref[:, 5 * D + c * SCATTER_COLS:5 * D + (c + 1) * SCATTER_COLS] * acc
    if final:
        out_ref[...] = _rms(out_ref[...], fg_ref[...])


def _scatter(offs, pos, aff, ye, x1, mods_l, fg, n, row0, group_of, final=False):
    cap = EC_CAPACITY * n // N_EXPERTS
    win = min(cap, 2 * STOK)
    tiles = n // STOK
    blk0 = row0 // STOK
    if tiles % 8 == 0:
        prows, pidx = tiles, lambda b: b
    else:
        prows, pidx = BATCH * tiles, lambda b: 0
    grid_spec = pltpu.PrefetchScalarGridSpec(
        num_scalar_prefetch=1,
        grid=(BATCH, tiles),
        in_specs=[
            pl.BlockSpec((N_EXPERTS, prows, STOK), lambda b, t, *_: (0, pidx(b), 0)),
            pl.BlockSpec((N_EXPERTS, prows, STOK), lambda b, t, *_: (0, pidx(b) + blk0 // prows, 0)),
            pl.BlockSpec((N_EXPERTS, cap, D), lambda b, t, *_: (0, b, 0)),
            pl.BlockSpec((STOK, D), lambda b, t, *_: (blk0 + b * tiles + t, 0)),
            pl.BlockSpec((None, 1, N_MOD * D), lambda b, t, *_: (group_of(b), 0, 0)),
            pl.BlockSpec((1, D), lambda b, t, *_: (0, 0)),
        ],
        out_specs=pl.BlockSpec((STOK, D), lambda b, t, *_: (b * tiles + t, 0)),
        scratch_shapes=[pltpu.VMEM((N_EXPERTS, STOK, win) if win >= LANE else (N_EXPERTS, win, STOK), BF16)],
    )
    return pl.pallas_call(
        functools.partial(_scatter_kernel, n=n, cap=cap, final=final),
        grid_spec=grid_spec,
        out_shape=jax.ShapeDtypeStruct((BATCH * n, D), F32),
        compiler_params=_cparams(("arbitrary", "arbitrary")),
        name="scatter_%d" % n,
    )(offs, pos, aff, ye, x1, mods_l, fg)


def _prefix_weights():
    k = jnp.arange(LANE)
    upper = (k[:, None] <= k[None, :]).astype(BF16)
    return jnp.concatenate([upper, jnp.ones((LANE, LANE), BF16)], axis=1)


def kernel(x, c, ctx, c_ctx, ada_w, ada_b, norm_mix_g, norm_ffn_g, a_w_in, a_ln_g, a_ln_b, a_w_s, a_b_s,
           a_w_out, r_w_in, r_decay_f, r_decay_b, r_w_out, moe_w_router, moe_w_gate, moe_w_up, moe_w_down,
           final_norm_g):
    x_lat = x.reshape(N_LAT, D)
    x_ctx = ctx.reshape(N_CTX, D)
    cc = jnp.concatenate([c, c_ctx[None], jnp.zeros((MOD_ROWS - BATCH - 1, D), F32)], axis=0)
    mods = _mods(cc, ada_w, ada_b).reshape(DEPTH, MOD_ROWS, 1, N_MOD * D)
    cos_t, sin_t = _rope_tables()
    w2 = _prefix_weights()
    row = lambda v: v.reshape(1, -1)
    fg = row(final_norm_g)
    wg = moe_w_gate.astype(BF16)
    wu = moe_w_up.astype(BF16)
    wd = moe_w_down.astype(BF16)

    for i in range(DEPTH):
        last = i == DEPTH - 1
        j = i // 2
        gffn = row(norm_ffn_g[i])
        wrt = moe_w_router[i].T.astype(BF16)
        if i % 2 == 0:
            bs_full = jnp.repeat(a_b_s[j].T, CHUNK, axis=1)
            x1, h2, aff_t = _gmlp_layer(
                x_lat, x_ctx, mods[i], row(norm_mix_g[i]), a_w_in[j].astype(BF16), row(a_ln_g[j]),
                row(a_ln_b[j]), a_w_s[j].astype(BF16), bs_full, a_w_out[j].astype(BF16), gffn, wrt)
        else:
            qkv, gates = _ret_proj(x_lat, x_ctx, mods[i], row(norm_mix_g[i]), r_w_in[j].astype(BF16),
                                   cos_t, sin_t)
            cdec, intra, qdec, kdec = _decay_tables(r_decay_f[j], r_decay_b[j])
            x1, h2, aff_t = _ret_scan(qkv, gates, x_lat, x_ctx, mods[i], cdec, intra, qdec, kdec,
                                      r_w_out[j].astype(BF16), gffn, wrt)
        aff3 = aff_t.reshape(N_EXPERTS, TOK // STOK, STOK)

        pos, offs = _route(aff_t, w2, SEQ, 0)
        offs = offs.reshape(-1)
        pos = pos.reshape(N_EXPERTS, N_LAT // LANE, LANE)
        ye = _gather_ffn(offs, pos, h2, 0, i, wg, wu, wd, SEQ, 1)
        x_lat = _scatter(offs, pos, aff3, ye, x1, mods[i], fg, SEQ, 0, lambda b: b, final=last)
        if last:
            return x_lat.reshape(BATCH, SEQ, D)
        pos, offs = _route(aff_t, w2, CTX_LEN, N_LAT)
        offs = offs.reshape(-1)
        pos = pos.reshape(N_EXPERTS, N_CTX // LANE, LANE)
        ye = _gather_ffn(offs, pos, h2, N_LAT, i, wg, wu, wd, CTX_LEN, BATCH)
        x_ctx = _scatter(offs, pos, aff3, ye, x1, mods[i], fg, CTX_LEN, N_LAT, lambda b: CTX_GROUP)
```

```python
import functools

import jax
import jax.numpy as jnp
import numpy as np
from jax import lax
from jax.experimental import pallas as pl
from jax.experimental.pallas import tpu as pltpu

D = 1024
BATCH = 8
SEQ = 4096
DEPTH = 4
GRID_W = 64
CTX_LEN = 256
N_MOD = 6
CHUNK = 128
RET_CHUNK = 256
LANE = 128
A_GROUPS = 8
HEADS = 4
DK = 256
DV = 256
ROPE_BASE = 10000.0
N_EXPERTS = 16
EC_CAPACITY = 2
EPS = 1e-6

N_LAT = BATCH * SEQ
N_CTX = BATCH * CTX_LEN
TOK = N_LAT + N_CTX
TM = 256
LAT_TILES = N_LAT // TM
TILES_PER_SAMPLE = SEQ // TM
TMD = 512
LAT_DTILES = N_LAT // TMD
DTILES = TOK // TMD
DTILES_PER_SAMPLE = SEQ // TMD
CTX_GROUP = BATCH
MOD_ROWS = 16

F32 = jnp.float32
BF16 = jnp.bfloat16
VMEM_LIMIT = 48 * 1024 * 1024


def _cparams(sem):
    return pltpu.CompilerParams(dimension_semantics=sem, vmem_limit_bytes=VMEM_LIMIT)


def _rms(x, g):
    return x * lax.rsqrt(jnp.mean(x * x, axis=-1, keepdims=True) + EPS) * g


def _silu(x):
    return x * (1.0 / (1.0 + jnp.exp(-x)))


def _gelu_tanh(x):
    return 0.5 * x * (1.0 + jnp.tanh(0.7978845608028654 * (x + 0.044715 * (x * x * x))))


def _post(x, y, mod, gffn, wrt):
    g1 = mod[:, 2 * D:3 * D]
    sh2 = mod[:, 3 * D:4 * D]
    sc2 = mod[:, 4 * D:5 * D]
    x1 = x + g1 * y
    h2 = (_rms(x1, gffn) * (1.0 + sc2) + sh2).astype(BF16)
    logits = lax.dot_general(wrt, h2, (((1,), (1,)), ((), ())), preferred_element_type=F32)
    m = jnp.max(logits, axis=0, keepdims=True)
    e = jnp.exp(logits - m)
    aff = e / jnp.sum(e, axis=0, keepdims=True)
    return x1, h2, aff


def _mods_kernel(cc_ref, w_ref, b_ref, o_ref):
    a = _silu(cc_ref[...]).astype(BF16)
    o_ref[...] = jnp.dot(a, w_ref[...].astype(BF16), preferred_element_type=F32) + b_ref[...]


def _mods(cc, ada_w, ada_b):
    tn = 1536
    return pl.pallas_call(
        _mods_kernel,
        grid=(DEPTH, N_MOD * D // tn),
        in_specs=[
            pl.BlockSpec((MOD_ROWS, D), lambda i, n: (0, 0)),
            pl.BlockSpec((None, D, tn), lambda i, n: (i, 0, n)),
            pl.BlockSpec((None, 1, tn), lambda i, n: (i, 0, n)),
        ],
        out_specs=pl.BlockSpec((None, MOD_ROWS, tn), lambda i, n: (i, 0, n)),
        out_shape=jax.ShapeDtypeStruct((DEPTH, MOD_ROWS, N_MOD * D), F32),
        compiler_params=_cparams(("parallel", "parallel")),
        name="adaln_mods",
    )(cc, ada_w, ada_b.reshape(DEPTH, 1, N_MOD * D))


def _tile_group(i):
    return jnp.minimum(i // DTILES_PER_SAMPLE, CTX_GROUP)


def _gmlp_kernel(xl_ref, xc_ref, mod_ref, gmix_ref, win_ref, lng_ref, lnb_ref, ws_ref, bs_ref, wout_ref,
                 gffn_ref, wrt_ref, x1_ref, h2_ref, aff_ref):
    x = jnp.where(pl.program_id(0) < LAT_DTILES, xl_ref[...], xc_ref[...])
    mod = mod_ref[...]
    h = (_rms(x, gmix_ref[...]) * (1.0 + mod[:, D:2 * D]) + mod[:, 0:D]).astype(BF16)
    z = _gelu_tanh(jnp.dot(h, win_ref[...], preferred_element_type=F32))
    u = z[:, :D]
    v = z[:, D:]
    mu = jnp.mean(v, axis=-1, keepdims=True)
    vc = v - mu
    var = jnp.mean(vc * vc, axis=-1, keepdims=True)
    v = (vc * lax.rsqrt(var + EPS) * lng_ref[...] + lnb_ref[...]).astype(BF16)
    bs = bs_ref[...]
    rows = []
    for c in range(TMD // CHUNK):
        cols = []
        for g in range(A_GROUPS):
            vg = v[c * CHUNK:(c + 1) * CHUNK, g * CHUNK:(g + 1) * CHUNK]
            s = jnp.dot(ws_ref[g], vg, preferred_element_type=F32)
            cols.append(s + bs[:, g * CHUNK:(g + 1) * CHUNK])
        rows.append(jnp.concatenate(cols, axis=1))
    s = jnp.concatenate(rows, axis=0)
    y = jnp.dot((u * s).astype(BF16), wout_ref[...], preferred_element_type=F32)
    x1, h2, aff = _post(x, y, mod, gffn_ref[...], wrt_ref[...])
    x1_ref[...] = x1
    h2_ref[...] = h2
    aff_ref[...] = aff


def _gmlp_layer(x_lat, x_ctx, mods_l, gmix, win, lng, lnb, ws, bs_full, wout, gffn, wrt):
    full = lambda shape: pl.BlockSpec(shape, lambda i: (0,) * len(shape))
    return pl.pallas_call(
        _gmlp_kernel,
        grid=(DTILES,),
        in_specs=[
            pl.BlockSpec((TMD, D), lambda i: (jnp.minimum(i, LAT_DTILES - 1), 0)),
            pl.BlockSpec((TMD, D), lambda i: (jnp.maximum(i - LAT_DTILES, 0), 0)),
            pl.BlockSpec((None, 1, N_MOD * D), lambda i: (_tile_group(i), 0, 0)),
            full((1, D)), full((D, 2 * D)), full((1, D)), full((1, D)),
            full((A_GROUPS, CHUNK, CHUNK)), full((CHUNK, D)), full((D, D)),
            full((1, D)), full((N_EXPERTS, D)),
        ],
        out_specs=[
            pl.BlockSpec((TMD, D), lambda i: (i, 0)),
            pl.BlockSpec((TMD, D), lambda i: (i, 0)),
            pl.BlockSpec((N_EXPERTS, TMD), lambda i: (0, i)),
        ],
        out_shape=[
            jax.ShapeDtypeStruct((TOK, D), F32),
            jax.ShapeDtypeStruct((TOK, D), BF16),
            jax.ShapeDtypeStruct((N_EXPERTS, TOK), F32),
        ],
        compiler_params=_cparams(("parallel",)),
        name="gmlp_mixer",
    )(x_lat, x_ctx, mods_l, gmix, win, lng, lnb, ws, bs_full, wout, gffn, wrt)


def _ret_proj_kernel(xl_ref, xc_ref, mod_ref, gmix_ref, win_ref, cos_ref, sin_ref, qkv_ref, gate_ref):
    x = jnp.where(pl.program_id(0) < LAT_DTILES, xl_ref[...], xc_ref[...])
    mod = mod_ref[...]
    h = (_rms(x, gmix_ref[...]) * (1.0 + mod[:, D:2 * D]) + mod[:, 0:D]).astype(BF16)
    z = jnp.dot(h, win_ref[...], preferred_element_type=F32)
    cos = cos_ref[...]
    sin = sin_ref[...]
    half = DK // 2
    for part, scale in ((0, 1.0), (1, DK ** -0.5)):
        for hd in range(HEADS):
            base = part * D + hd * DK
            t1 = z[:, base:base + half]
            t2 = z[:, base + half:base + DK]
            qkv_ref[:, base:base + half] = ((t1 * cos - t2 * sin) * scale).astype(BF16)
            qkv_ref[:, base + half:base + DK] = ((t1 * sin + t2 * cos) * scale).astype(BF16)
    qkv_ref[:, 2 * D:3 * D] = z[:, 2 * D:3 * D].astype(BF16)
    gate_ref[...] = _silu(z[:, 3 * D:5 * D]).astype(BF16)


def _ret_proj(x_lat, x_ctx, mods_l, gmix, win, cos_t, sin_t):
    full = lambda shape: pl.BlockSpec(shape, lambda i: (0,) * len(shape))
    rope_blk = lambda i: (jnp.where(i < LAT_DTILES, i % DTILES_PER_SAMPLE, DTILES_PER_SAMPLE), 0)
    return pl.pallas_call(
        _ret_proj_kernel,
        grid=(DTILES,),
        in_specs=[
            pl.BlockSpec((TMD, D), lambda i: (jnp.minimum(i, LAT_DTILES - 1), 0)),
            pl.BlockSpec((TMD, D), lambda i: (jnp.maximum(i - LAT_DTILES, 0), 0)),
            pl.BlockSpec((None, 1, N_MOD * D), lambda i: (_tile_group(i), 0, 0)),
            full((1, D)), full((D, 5 * D)),
            pl.BlockSpec((TMD, DK // 2), rope_blk),
            pl.BlockSpec((TMD, DK // 2), rope_blk),
        ],
        out_specs=[
            pl.BlockSpec((TMD, 3 * D), lambda i: (i, 0)),
            pl.BlockSpec((TMD, 2 * D), lambda i: (i, 0)),
        ],
        out_shape=[
            jax.ShapeDtypeStruct((TOK, 3 * D), BF16),
            jax.ShapeDtypeStruct((TOK, 2 * D), BF16),
        ],
        compiler_params=_cparams(("parallel",)),
        name="retention_proj",
    )(x_lat, x_ctx, mods_l, gmix, win, cos_t, sin_t)


SCAN_STEPS = TILES_PER_SAMPLE + 1


def _ret_scan_kernel(cdec_ref, qkv_ref, gate_ref, xl_ref, xc_ref, mod_ref, intra_ref, qdec_ref, kdec_ref,
                     wout_ref, gffn_ref, wrt_ref, x1_ref, h2_ref, aff_ref, s_ref, yf_ref):
    p = pl.program_id(1)
    j = pl.program_id(2)

    @pl.when(j == 0)
    def _():
        s_ref[...] = jnp.zeros_like(s_ref)

    def chunk_out(c):
        r0 = c * RET_CHUNK
        outs = []
        wide = lambda t: jnp.concatenate([t] * (DK // LANE), axis=1)
        for hd in range(HEADS):
            q = qkv_ref[r0:r0 + RET_CHUNK, hd * DK:(hd + 1) * DK]
            k = qkv_ref[r0:r0 + RET_CHUNK, D + hd * DK:D + (hd + 1) * DK]
            v = qkv_ref[r0:r0 + RET_CHUNK, 2 * D + hd * DV:2 * D + (hd + 1) * DV]
            att = lax.dot_general(q, k, (((1,), (1,)), ((), ())), preferred_element_type=F32)
            att = (att * intra_ref[hd]).astype(BF16)
            s_old = s_ref[hd]
            o = jnp.dot(att, v, preferred_element_type=F32)
            o = o + jnp.dot(q, s_old.astype(BF16), preferred_element_type=F32) * wide(qdec_ref[hd])
            ks = (k.astype(F32) * wide(kdec_ref[hd])).astype(BF16)
            kv = lax.dot_general(ks, v, (((0,), (0,)), ((), ())), preferred_element_type=F32)
            s_ref[hd] = cdec_ref[p, hd] * s_old + kv
            mu = jnp.mean(o, axis=-1, keepdims=True)
            oc = o - mu
            var = jnp.mean(oc * oc, axis=-1, keepdims=True)
            gate = gate_ref[r0:r0 + RET_CHUNK, hd * DV:(hd + 1) * DV].astype(F32)
            outs.append(gate * (oc * lax.rsqrt(var + EPS)))
        return jnp.concatenate(outs, axis=1)

    @pl.when(p == 0)
    def _():
        row = pl.multiple_of(j * TM, TM)
        for c in range(TM // RET_CHUNK):
            yf_ref[pl.ds(row + c * RET_CHUNK, RET_CHUNK), :] = chunk_out(c)

    @pl.when(p == 1)
    def _():
        pos = jnp.where(j == 0, 0, SCAN_STEPS - j)
        row = pl.multiple_of(pos * TM, TM)
        ys = [None] * (TM // RET_CHUNK)
        for c in reversed(range(TM // RET_CHUNK)):
            ys[c] = chunk_out(c) + yf_ref[pl.ds(row + c * RET_CHUNK, RET_CHUNK), :]
        y = jnp.concatenate(ys, axis=0).astype(BF16)
        y = jnp.dot(y, wout_ref[...], preferred_element_type=F32)
        x = jnp.where(j == 0, xc_ref[...], xl_ref[...])
        x1, h2, aff = _post(x, y, mod_ref[...], gffn_ref[...], wrt_ref[...])
        x1_ref[...] = x1
        h2_ref[...] = h2
        aff_ref[...] = aff


def _scan_tile(b, p, j):
    lat = b * TILES_PER_SAMPLE + jnp.where(p == 0, j - 1, TILES_PER_SAMPLE - j)
    return jnp.where(j == 0, LAT_TILES + b, lat)


def _ret_scan(qkv, gates, x_lat, x_ctx, mods_l, cdec, intra, qdec, kdec, wout, gffn, wrt):
    full = lambda shape: pl.BlockSpec(shape, lambda b, p, j, *_: (0,) * len(shape))
    tile = lambda b, p, j, *_: (_scan_tile(b, p, j), 0)
    tile_p1 = lambda b, p, j, *_: (_scan_tile(b, 1, jnp.where(p == 0, 0, j)), 0)
    tile_p1_t = lambda b, p, j, *_: (0, _scan_tile(b, 1, jnp.where(p == 0, 0, j)))
    lat_p1 = lambda b, p, j, *_: (b * TILES_PER_SAMPLE + TILES_PER_SAMPLE - jnp.where(p == 0, 1, jnp.maximum(j, 1)), 0)
    per_dir = lambda shape: pl.BlockSpec((None,) + shape, lambda b, p, j, *_: (p,) + (0,) * len(shape))
    grid_spec = pltpu.PrefetchScalarGridSpec(
        num_scalar_prefetch=0,
        grid=(BATCH, 2, SCAN_STEPS),
        in_specs=[
            pl.BlockSpec(memory_space=pltpu.SMEM),
            pl.BlockSpec((TM, 3 * D), tile),
            pl.BlockSpec((TM, D), lambda b, p, j, *_: (_scan_tile(b, p, j), p)),
            pl.BlockSpec((TM, D), lat_p1),
            pl.BlockSpec((TM, D), lambda b, p, j, *_: (b, 0)),
            pl.BlockSpec((None, 1, N_MOD * D), lambda b, p, j, *_: (jnp.where(j == 0, CTX_GROUP, b), 0, 0)),
            per_dir((HEADS, RET_CHUNK, RET_CHUNK)), per_dir((HEADS, RET_CHUNK, LANE)),
            per_dir((HEADS, RET_CHUNK, LANE)),
            full((D, D)), full((1, D)), full((N_EXPERTS, D)),
        ],
        out_specs=[
            pl.BlockSpec((TM, D), tile_p1),
            pl.BlockSpec((TM, D), tile_p1),
            pl.BlockSpec((N_EXPERTS, TM), tile_p1_t),
        ],
        scratch_shapes=[
            pltpu.VMEM((HEADS, DK, DV), F32),
            pltpu.VMEM((SCAN_STEPS * TM, D), F32),
        ],
    )
    return pl.pallas_call(
        _ret_scan_kernel,
        grid_spec=grid_spec,
        out_shape=[
            jax.ShapeDtypeStruct((TOK, D), F32),
            jax.ShapeDtypeStruct((TOK, D), BF16),
            jax.ShapeDtypeStruct((N_EXPERTS, TOK), F32),
        ],
        compiler_params=_cparams(("arbitrary", "arbitrary", "arbitrary")),
        name="retention_scan",
    )(cdec, qkv, gates, x_lat, x_ctx, mods_l, intra, qdec, kdec, wout, gffn, wrt)


def _decay_tables(decay_f, decay_b):
    idx = jnp.arange(RET_CHUNK, dtype=F32)
    diff = idx[:, None] - idx[None, :]
    lg_f = jax.nn.log_sigmoid(decay_f.astype(F32))
    lg_b = jax.nn.log_sigmoid(decay_b.astype(F32))
    mask_f = diff >= 0
    intra_f = jnp.where(mask_f, jnp.exp(jnp.where(mask_f, diff, 0.0)[None] * lg_f[:, None, None]), 0.0)
    mask_b = diff < 0
    intra_b = jnp.where(mask_b, jnp.exp(jnp.where(mask_b, -diff, 0.0)[None] * lg_b[:, None, None]), 0.0)
    qdec_f = jnp.exp((idx + 1.0)[None] * lg_f[:, None])
    kdec_f = jnp.exp((RET_CHUNK - 1.0 - idx)[None] * lg_f[:, None])
    qdec_b = jnp.exp((RET_CHUNK - idx)[None] * lg_b[:, None])
    kdec_b = jnp.exp(idx[None] * lg_b[:, None])
    wide = lambda t: jnp.broadcast_to(t[:, :, None], (HEADS, RET_CHUNK, LANE))
    intra = jnp.stack([intra_f, intra_b])
    qdec = jnp.stack([wide(qdec_f), wide(qdec_b)])
    kdec = jnp.stack([wide(kdec_f), wide(kdec_b)])
    cdec = jnp.stack([jnp.exp(RET_CHUNK * lg_f), jnp.exp(RET_CHUNK * lg_b)])
    return cdec, intra, qdec, kdec


def _rope_tables():
    rows = SEQ // GRID_W
    pos_r = np.repeat(np.arange(rows), GRID_W).astype(np.float64)
    pos_c = np.tile(np.arange(GRID_W), rows).astype(np.float64)
    n_freq = DK // 4
    inv = np.power(ROPE_BASE, -np.arange(n_freq, dtype=np.float64) / n_freq)
    ang = np.concatenate([pos_r[:, None] * inv[None], pos_c[:, None] * inv[None]], axis=-1)
    cos = np.concatenate([np.cos(ang), np.ones((TMD, DK // 2))], axis=0)
    sin = np.concatenate([np.sin(ang), np.zeros((TMD, DK // 2))], axis=0)
    return jnp.asarray(cos, F32), jnp.asarray(sin, F32)


KEY_ONE = 0x3F800000
BISECT_STEPS = 31
REFINE_STEPS = 12


def _route_kernel(aff_ref, w2_ref, pos_ref, offs_ref, *, n, cap):
    sample = lambda ref, s: ref[:, s * n:(s + 1) * n]
    count = lambda m: jnp.sum(jnp.where(m, 1.0, 0.0), axis=1, keepdims=True)

    def bisect(_, carry):
        nxt = []
        for s in range(BATCH):
            lo, hi = carry[s]
            mid = lo + ((hi - lo + 1) >> 1)
            ok = count(sample(aff_ref, s) >= pltpu.bitcast(mid, F32)) >= cap
            nxt.append((jnp.where(ok, mid, lo), jnp.where(ok, hi, mid - 1)))
        return tuple(nxt)

    lo0 = jnp.zeros((N_EXPERTS, 1), jnp.int32)
    hi0 = jnp.full((N_EXPERTS, 1), KEY_ONE, jnp.int32)
    keys = lax.fori_loop(0, BISECT_STEPS, bisect, ((lo0, hi0),) * BATCH)

    def refine(_, carry):
        nxt = []
        for s in range(BATCH):
            a, b = carry[s]
            m = a + (b - a) * 0.5
            ok = count(sample(aff_ref, s) >= m) >= cap
            nxt.append((jnp.where(ok, m, a), jnp.where(ok, b, m)))
        return tuple(nxt)

    cuts = lax.fori_loop(0, REFINE_STEPS, refine,
                         tuple((pltpu.bitcast(k, F32), pltpu.bitcast(k + 1, F32)) for k, _ in keys))

    w2 = w2_ref[...]
    lane = lax.broadcasted_iota(jnp.int32, (N_EXPERTS, LANE), 1)

    nb = n // LANE
    sblk = min(cap, LANE)

    def prefix(mask):
        carry = jnp.zeros((N_EXPERTS, LANE), F32)
        offs = jnp.zeros((N_EXPERTS, LANE), F32)
        first = [jnp.zeros((N_EXPERTS, LANE), F32) for _ in range(cap // sblk)]
        incs = []
        for j in range(nb):
            blk = mask[:, j * LANE:(j + 1) * LANE].astype(BF16)
            r = jnp.dot(blk, w2, preferred_element_type=F32)
            incs.append(r[:, :LANE] + carry)
            offs = jnp.where(lane == j, carry, offs)
            carry = carry + r[:, LANE:]
            first = [f + jnp.where(carry <= k * sblk, 1.0, 0.0) for k, f in enumerate(first)]
        offs = jnp.where(lane == nb, carry, offs)
        for k, f in enumerate(first):
            offs = jnp.where(lane == nb + 1 + k, f, offs)
        return jnp.concatenate(incs, axis=1), offs

    for s in range(BATCH):
        aff = sample(aff_ref, s)
        thr, above = cuts[s]
        gt = jnp.where(aff >= above, 1.0, 0.0)
        eq = jnp.where(aff >= thr, 1.0, 0.0) - gt
        need = cap - jnp.sum(gt, axis=1, keepdims=True)
        tie_inc, _ = prefix(eq)
        sel = gt + eq * jnp.where(tie_inc <= need, 1.0, 0.0)
        inc, offs = prefix(sel)
        pos_ref[:, s * n:(s + 1) * n] = jnp.where(sel > 0.0, inc.astype(jnp.int32) - 1, -1)
        offs_ref[s] = offs.astype(jnp.int32)


def _route(aff_t, w2, n, col0):
    cap = EC_CAPACITY * n // N_EXPERTS
    return pl.pallas_call(
        functools.partial(_route_kernel, n=n, cap=cap),
        grid=(1,),
        in_specs=[pl.BlockSpec((N_EXPERTS, BATCH * n), lambda i: (0, col0 // (BATCH * n))),
                  pl.BlockSpec((LANE, 2 * LANE), lambda i: (0, 0))],
        out_specs=[pl.BlockSpec((N_EXPERTS, BATCH * n), lambda i: (0, 0)),
                   pl.BlockSpec((BATCH, N_EXPERTS, LANE), lambda i: (0, 0, 0))],
        out_shape=[jax.ShapeDtypeStruct((N_EXPERTS, BATCH * n), jnp.int32),
                   jax.ShapeDtypeStruct((BATCH, N_EXPERTS, LANE), jnp.int32)],
        compiler_params=_cparams(("arbitrary",)),
        name="route_%d" % n,
    )(aff_t, w2)


GATHER_WINDOW = 1280


def _gather_ffn_kernel(offs_ref, pos_ref, h2_ref, wg_ref, wu_ref, wd_ref, ye_ref, xs_ref, *, n, cap, spb):
    e = pl.program_id(0)
    b0 = pl.program_id(1) * spb
    sblk = min(cap, LANE)
    nb = n // LANE
    wb = min(GATHER_WINDOW, n) // LANE
    for s in range(spb):
        base = ((b0 + s) * N_EXPERTS + e) * LANE
        for sb in range(cap // sblk):
            s0 = sb * sblk
            rows = slice(s * cap + s0, s * cap + s0 + sblk)
            slot = s0 + lax.broadcasted_iota(jnp.int32, (sblk, LANE), 0)

            def window(first_blk, lo_slot):
                start = jnp.minimum(first_blk, nb - wb) if wb < nb else 0
                pieces = []
                for i in range(wb):
                    row = pos_ref[e, pl.ds(s * nb + start + i, 1), :]
                    row = jnp.where(row >= lo_slot, row, -1)
                    pieces.append(jnp.where(row == slot, 1.0, 0.0).astype(BF16))
                onehot = jnp.concatenate(pieces, axis=1)
                tok0 = pl.multiple_of((s * nb + start) * LANE, LANE)
                got = jnp.dot(onehot, h2_ref[pl.ds(tok0, wb * LANE), :], preferred_element_type=F32)
                return got.astype(BF16), start + wb

            got, nxt = window(offs_ref[base + nb + 1 + sb], s0)
            xs_ref[rows, :] = got
            if wb < nb:
                def more(nxt):
                    return jnp.logical_and(nxt < nb, offs_ref[base + jnp.minimum(nxt, nb)] < s0 + sblk)

                def extra(nxt):
                    got, nxt2 = window(nxt, offs_ref[base + nxt])
                    xs_ref[rows, :] = xs_ref[rows, :] + got
                    return nxt2

                lax.while_loop(more, extra, nxt)
    xs = xs_ref[...]
    a = jnp.dot(xs, wg_ref[...], preferred_element_type=F32)
    u = jnp.dot(xs, wu_ref[...], preferred_element_type=F32)
    hid = (_silu(a) * u).astype(BF16)
    ye_ref[...] = jnp.dot(hid, wd_ref[...], preferred_element_type=F32).astype(BF16)


def _gather_ffn(offs, pos, h2, row0, layer, wg, wu, wd, n, spb):
    cap = EC_CAPACITY * n // N_EXPERTS
    wspec = pl.BlockSpec((None, None, D, D), lambda e, b, *_: (layer, e, 0, 0))
    grid_spec = pltpu.PrefetchScalarGridSpec(
        num_scalar_prefetch=1,
        grid=(N_EXPERTS, BATCH // spb),
        in_specs=[
            pl.BlockSpec((N_EXPERTS, spb * n // LANE, LANE), lambda e, b, *_: (0, b, 0)),
            pl.BlockSpec((spb * n, D), lambda e, b, *_: (b + row0 // (spb * n), 0)),
            wspec, wspec, wspec,
        ],
        out_specs=pl.BlockSpec((None, spb * cap, D), lambda e, b, *_: (e, b, 0)),
        scratch_shapes=[pltpu.VMEM((spb * cap, D), BF16)],
    )
    return pl.pallas_call(
        functools.partial(_gather_ffn_kernel, n=n, cap=cap, spb=spb),
        grid_spec=grid_spec,
        out_shape=jax.ShapeDtypeStruct((N_EXPERTS, BATCH * cap, D), BF16),
        compiler_params=_cparams(("arbitrary", "arbitrary")),
        name="gather_ffn_%d" % n,
    )(offs, pos, h2, wg, wu, wd)


STOK = 128
SCATTER_TILES = 2
SCATTER_COLS = 256


def _scatter_kernel(offs_ref, pos_ref, aff_ref, ye_ref, x1_ref, mod_ref, fg_ref, out_ref, pgt_ref,
                    *, n, cap, final):
    b = pl.program_id(0)
    tiles = n // STOK
    win = min(cap, 2 * STOK)
    shift = STOK.bit_length() - 1

    def window_start(e, t):
        if win == cap:
            return 0
        o0 = offs_ref[(b * N_EXPERTS + e) * LANE + t]
        return pl.multiple_of(jnp.minimum((o0 >> shift) << shift, cap - win), STOK)

    for u in range(SCATTER_TILES):
        t = pl.program_id(1) * SCATTER_TILES + u
        prow = t if tiles % 8 == 0 else b * tiles + t
        rows = slice(u * STOK, (u + 1) * STOK)
        for e in range(N_EXPERTS):
            row = pos_ref[e, pl.ds(prow, 1), :]
            gate = aff_ref[e, pl.ds(prow, 1), :]
            slot = window_start(e, t) + lax.broadcasted_iota(jnp.int32, (win, STOK), 0)
            pg = jnp.where(row == slot, gate, 0.0)
            pgt_ref[u, e] = pg.T.astype(BF16) if win >= LANE else pg.astype(BF16)
        for c in range(D // SCATTER_COLS):
            cols = slice(c * SCATTER_COLS, (c + 1) * SCATTER_COLS)
            acc = jnp.zeros((STOK, SCATTER_COLS), F32)
            for e in range(N_EXPERTS):
                yb = ye_ref[e, pl.ds(window_start(e, t), win), cols]
                if win >= LANE:
                    acc = acc + jnp.dot(pgt_ref[u, e], yb, preferred_element_type=F32)
                else:
                    acc = acc + lax.dot_general(pgt_ref[u, e], yb, (((0,), (0,)), ((), ())),
                                                preferred_element_type=F32)
            g2 = mod_ref[:, 5 * D + c * SCATTER_COLS:5 * D + (c + 1) * SCATTER_COLS]
            out_ref[rows, cols] = x1_ref[rows, cols] + g2 * acc
    if final:
        out_ref[...] = _rms(out_ref[...], fg_ref[...])


def _scatter(offs, pos, aff, ye, x1, mods_l, fg, n, row0, group_of, final=False):
    cap = EC_CAPACITY * n // N_EXPERTS
    win = min(cap, 2 * STOK)
    tiles = n // STOK
    steps = tiles // SCATTER_TILES
    rows = SCATTER_TILES * STOK
    blk0 = row0 // rows
    if tiles % 8 == 0:
        prows, pidx = tiles, lambda b: b
    else:
        prows, pidx = BATCH * tiles, lambda b: 0
    pgt_shape = (STOK, win) if win >= LANE else (win, STOK)
    grid_spec = pltpu.PrefetchScalarGridSpec(
        num_scalar_prefetch=1,
        grid=(BATCH, steps),
        in_specs=[
            pl.BlockSpec((N_EXPERTS, prows, STOK), lambda b, t, *_: (0, pidx(b), 0)),
            pl.BlockSpec((N_EXPERTS, prows, STOK), lambda b, t, *_: (0, pidx(b) + (row0 // STOK) // prows, 0)),
            pl.BlockSpec((N_EXPERTS, cap, D), lambda b, t, *_: (0, b, 0)),
            pl.BlockSpec((rows, D), lambda b, t, *_: (blk0 + b * steps + t, 0)),
            pl.BlockSpec((None, 1, N_MOD * D), lambda b, t, *_: (group_of(b), 0, 0)),
            pl.BlockSpec((1, D), lambda b, t, *_: (0, 0)),
        ],
        out_specs=pl.BlockSpec((rows, D), lambda b, t, *_: (b * steps + t, 0)),
        scratch_shapes=[pltpu.VMEM((SCATTER_TILES, N_EXPERTS) + pgt_shape, BF16)],
    )
    return pl.pallas_call(
        functools.partial(_scatter_kernel, n=n, cap=cap, final=final),
        grid_spec=grid_spec,
        out_shape=jax.ShapeDtypeStruct((BATCH * n, D), F32),
        compiler_params=_cparams(("arbitrary", "arbitrary")),
        name="scatter_%d" % n,
    )(offs, pos, aff, ye, x1, mods_l, fg)


def _prefix_weights():
    k = jnp.arange(LANE)
    upper = (k[:, None] <= k[None, :]).astype(BF16)
    return jnp.concatenate([upper, jnp.ones((LANE, LANE), BF16)], axis=1)


def kernel(x, c, ctx, c_ctx, ada_w, ada_b, norm_mix_g, norm_ffn_g, a_w_in, a_ln_g, a_ln_b, a_w_s, a_b_s,
           a_w_out, r_w_in, r_decay_f, r_decay_b, r_w_out, moe_w_router, moe_w_gate, moe_w_up, moe_w_down,
           final_norm_g):
    x_lat = x.reshape(N_LAT, D)
    x_ctx = ctx.reshape(N_CTX, D)
    cc = jnp.concatenate([c, c_ctx[None], jnp.zeros((MOD_ROWS - BATCH - 1, D), F32)], axis=0)
    mods = _mods(cc, ada_w, ada_b).reshape(DEPTH, MOD_ROWS, 1, N_MOD * D)
    cos_t, sin_t = _rope_tables()
    w2 = _prefix_weights()
    row = lambda v: v.reshape(1, -1)
    fg = row(final_norm_g)
    wg = moe_w_gate.astype(BF16)
    wu = moe_w_up.astype(BF16)
    wd = moe_w_down.astype(BF16)

    for i in range(DEPTH):
        last = i == DEPTH - 1
        j = i // 2
        gffn = row(norm_ffn_g[i])
        wrt = moe_w_router[i].T.astype(BF16)
        if i % 2 == 0:
            bs_full = jnp.repeat(a_b_s[j].T, CHUNK, axis=1)
            x1, h2, aff_t = _gmlp_layer(
                x_lat, x_ctx, mods[i], row(norm_mix_g[i]), a_w_in[j].astype(BF16), row(a_ln_g[j]),
                row(a_ln_b[j]), a_w_s[j].astype(BF16), bs_full, a_w_out[j].astype(BF16), gffn, wrt)
        else:
            qkv, gates = _ret_proj(x_lat, x_ctx, mods[i], row(norm_mix_g[i]), r_w_in[j].astype(BF16),
                                   cos_t, sin_t)
            cdec, intra, qdec, kdec = _decay_tables(r_decay_f[j], r_decay_b[j])
            x1, h2, aff_t = _ret_scan(qkv, gates, x_lat, x_ctx, mods[i], cdec, intra, qdec, kdec,
                                      r_w_out[j].astype(BF16), gffn, wrt)
        aff3 = aff_t.reshape(N_EXPERTS, TOK // STOK, STOK)

        pos, offs = _route(aff_t, w2, SEQ, 0)
        offs = offs.reshape(-1)
        pos = pos.reshape(N_EXPERTS, N_LAT // LANE, LANE)
        ye = _gather_ffn(offs, pos, h2, 0, i, wg, wu, wd, SEQ, 1)
        x_lat = _scatter(offs, pos, aff3, ye, x1, mods[i], fg, SEQ, 0, lambda b: b, final=last)
        if last:
            return x_lat.reshape(BATCH, SEQ, D)
        pos, offs = _route(aff_t, w2, CTX_LEN, N_LAT)
        offs = offs.reshape(-1)
        pos = pos.reshape(N_EXPERTS, N_CTX // LANE, LANE)
        ye = _gather_ffn(offs, pos, h2, N_LAT, i, wg, wu, wd, CTX_LEN, BATCH)
        x_ctx = _scatter(offs, pos, aff3, ye, x1, mods[i], fg, CTX_LEN, N_LAT, lambda b: CTX_GROUP)
```

```python
import functools

import jax
import jax.numpy as jnp
import numpy as np
from jax import lax
from jax.experimental import pallas as pl
from jax.experimental.pallas import tpu as pltpu

D = 1024
BATCH = 8
SEQ = 4096
DEPTH = 4
GRID_W = 64
CTX_LEN = 256
N_MOD = 6
CHUNK = 128
RET_CHUNK = 256
LANE = 128
A_GROUPS = 8
HEADS = 4
DK = 256
DV = 256
ROPE_BASE = 10000.0
N_EXPERTS = 16
EC_CAPACITY = 2
EPS = 1e-6

N_LAT = BATCH * SEQ
N_CTX = BATCH * CTX_LEN
TOK = N_LAT + N_CTX
TM = 256
LAT_TILES = N_LAT // TM
TILES_PER_SAMPLE = SEQ // TM
TMD = 512
LAT_DTILES = N_LAT // TMD
DTILES = TOK // TMD
DTILES_PER_SAMPLE = SEQ // TMD
CTX_GROUP = BATCH
MOD_ROWS = 16

F32 = jnp.float32
BF16 = jnp.bfloat16
VMEM_LIMIT = 48 * 1024 * 1024


def _cparams(sem, vmem=VMEM_LIMIT):
    return pltpu.CompilerParams(dimension_semantics=sem, vmem_limit_bytes=vmem)


def _rms(x, g):
    return x * lax.rsqrt(jnp.mean(x * x, axis=-1, keepdims=True) + EPS) * g


def _silu(x):
    return x * (1.0 / (1.0 + jnp.exp(-x)))


def _gelu_tanh(x):
    return 0.5 * x * (1.0 + jnp.tanh(0.7978845608028654 * (x + 0.044715 * (x * x * x))))


def _post(x, y, mod, gffn, wrt):
    g1 = mod[:, 2 * D:3 * D]
    sh2 = mod[:, 3 * D:4 * D]
    sc2 = mod[:, 4 * D:5 * D]
    x1 = x + g1 * y
    h2 = (_rms(x1, gffn) * (1.0 + sc2) + sh2).astype(BF16)
    logits = lax.dot_general(wrt, h2, (((1,), (1,)), ((), ())), preferred_element_type=F32)
    m = jnp.max(logits, axis=0, keepdims=True)
    e = jnp.exp(logits - m)
    aff = e / jnp.sum(e, axis=0, keepdims=True)
    return x1, h2, aff


def _mods_kernel(cc_ref, w_ref, b_ref, o_ref):
    a = _silu(cc_ref[...]).astype(BF16)
    o_ref[...] = jnp.dot(a, w_ref[...].astype(BF16), preferred_element_type=F32) + b_ref[...]


def _mods(cc, ada_w, ada_b):
    tn = 1536
    return pl.pallas_call(
        _mods_kernel,
        grid=(DEPTH, N_MOD * D // tn),
        in_specs=[
            pl.BlockSpec((MOD_ROWS, D), lambda i, n: (0, 0)),
            pl.BlockSpec((None, D, tn), lambda i, n: (i, 0, n)),
            pl.BlockSpec((None, 1, tn), lambda i, n: (i, 0, n)),
        ],
        out_specs=pl.BlockSpec((None, MOD_ROWS, tn), lambda i, n: (i, 0, n)),
        out_shape=jax.ShapeDtypeStruct((DEPTH, MOD_ROWS, N_MOD * D), F32),
        compiler_params=_cparams(("parallel", "parallel")),
        name="adaln_mods",
    )(cc, ada_w, ada_b.reshape(DEPTH, 1, N_MOD * D))


def _tile_group(i):
    return jnp.minimum(i // DTILES_PER_SAMPLE, CTX_GROUP)


def _gmlp_kernel(xl_ref, xc_ref, mod_ref, gmix_ref, win_ref, lng_ref, lnb_ref, ws_ref, bs_ref, wout_ref,
                 gffn_ref, wrt_ref, x1_ref, h2_ref, aff_ref):
    x = jnp.where(pl.program_id(0) < LAT_DTILES, xl_ref[...], xc_ref[...])
    mod = mod_ref[...]
    h = (_rms(x, gmix_ref[...]) * (1.0 + mod[:, D:2 * D]) + mod[:, 0:D]).astype(BF16)
    z = _gelu_tanh(jnp.dot(h, win_ref[...], preferred_element_type=F32))
    u = z[:, :D]
    v = z[:, D:]
    mu = jnp.mean(v, axis=-1, keepdims=True)
    vc = v - mu
    var = jnp.mean(vc * vc, axis=-1, keepdims=True)
    v = (vc * lax.rsqrt(var + EPS) * lng_ref[...] + lnb_ref[...]).astype(BF16)
    bs = bs_ref[...]
    rows = []
    for c in range(TMD // CHUNK):
        cols = []
        for g in range(A_GROUPS):
            vg = v[c * CHUNK:(c + 1) * CHUNK, g * CHUNK:(g + 1) * CHUNK]
            s = jnp.dot(ws_ref[g], vg, preferred_element_type=F32)
            cols.append(s + bs[:, g * CHUNK:(g + 1) * CHUNK])
        rows.append(jnp.concatenate(cols, axis=1))
    s = jnp.concatenate(rows, axis=0)
    y = jnp.dot((u * s).astype(BF16), wout_ref[...], preferred_element_type=F32)
    x1, h2, aff = _post(x, y, mod, gffn_ref[...], wrt_ref[...])
    x1_ref[...] = x1
    h2_ref[...] = h2
    aff_ref[...] = aff


def _gmlp_layer(x_lat, x_ctx, mods_l, gmix, win, lng, lnb, ws, bs_full, wout, gffn, wrt):
    full = lambda shape: pl.BlockSpec(shape, lambda i: (0,) * len(shape))
    return pl.pallas_call(
        _gmlp_kernel,
        grid=(DTILES,),
        in_specs=[
            pl.BlockSpec((TMD, D), lambda i: (jnp.minimum(i, LAT_DTILES - 1), 0)),
            pl.BlockSpec((TMD, D), lambda i: (jnp.maximum(i - LAT_DTILES, 0), 0)),
            pl.BlockSpec((None, 1, N_MOD * D), lambda i: (_tile_group(i), 0, 0)),
            full((1, D)), full((D, 2 * D)), full((1, D)), full((1, D)),
            full((A_GROUPS, CHUNK, CHUNK)), full((CHUNK, D)), full((D, D)),
            full((1, D)), full((N_EXPERTS, D)),
        ],
        out_specs=[
            pl.BlockSpec((TMD, D), lambda i: (i, 0)),
            pl.BlockSpec((TMD, D), lambda i: (i, 0)),
            pl.BlockSpec((N_EXPERTS, TMD), lambda i: (0, i)),
        ],
        out_shape=[
            jax.ShapeDtypeStruct((TOK, D), F32),
            jax.ShapeDtypeStruct((TOK, D), BF16),
            jax.ShapeDtypeStruct((N_EXPERTS, TOK), F32),
        ],
        compiler_params=_cparams(("parallel",)),
        name="gmlp_mixer",
    )(x_lat, x_ctx, mods_l, gmix, win, lng, lnb, ws, bs_full, wout, gffn, wrt)


def _ret_proj_kernel(xl_ref, xc_ref, mod_ref, gmix_ref, win_ref, cos_ref, sin_ref, qkv_ref, gate_ref):
    x = jnp.where(pl.program_id(0) < LAT_DTILES, xl_ref[...], xc_ref[...])
    mod = mod_ref[...]
    h = (_rms(x, gmix_ref[...]) * (1.0 + mod[:, D:2 * D]) + mod[:, 0:D]).astype(BF16)
    z = jnp.dot(h, win_ref[...], preferred_element_type=F32)
    cos = cos_ref[...]
    sin = sin_ref[...]
    half = DK // 2
    for part, scale in ((0, 1.0), (1, DK ** -0.5)):
        for hd in range(HEADS):
            base = part * D + hd * DK
            t1 = z[:, base:base + half]
            t2 = z[:, base + half:base + DK]
            qkv_ref[:, base:base + half] = ((t1 * cos - t2 * sin) * scale).astype(BF16)
            qkv_ref[:, base + half:base + DK] = ((t1 * sin + t2 * cos) * scale).astype(BF16)
    qkv_ref[:, 2 * D:3 * D] = z[:, 2 * D:3 * D].astype(BF16)
    gate_ref[...] = _silu(z[:, 3 * D:5 * D]).astype(BF16)


def _ret_proj(x_lat, x_ctx, mods_l, gmix, win, cos_t, sin_t):
    full = lambda shape: pl.BlockSpec(shape, lambda i: (0,) * len(shape))
    rope_blk = lambda i: (jnp.where(i < LAT_DTILES, i % DTILES_PER_SAMPLE, DTILES_PER_SAMPLE), 0)
    return pl.pallas_call(
        _ret_proj_kernel,
        grid=(DTILES,),
        in_specs=[
            pl.BlockSpec((TMD, D), lambda i: (jnp.minimum(i, LAT_DTILES - 1), 0)),
            pl.BlockSpec((TMD, D), lambda i: (jnp.maximum(i - LAT_DTILES, 0), 0)),
            pl.BlockSpec((None, 1, N_MOD * D), lambda i: (_tile_group(i), 0, 0)),
            full((1, D)), full((D, 5 * D)),
            pl.BlockSpec((TMD, DK // 2), rope_blk),
            pl.BlockSpec((TMD, DK // 2), rope_blk),
        ],
        out_specs=[
            pl.BlockSpec((TMD, 3 * D), lambda i: (i, 0)),
            pl.BlockSpec((TMD, 2 * D), lambda i: (i, 0)),
        ],
        out_shape=[
            jax.ShapeDtypeStruct((TOK, 3 * D), BF16),
            jax.ShapeDtypeStruct((TOK, 2 * D), BF16),
        ],
        compiler_params=_cparams(("parallel",)),
        name="retention_proj",
    )(x_lat, x_ctx, mods_l, gmix, win, cos_t, sin_t)


SCAN_STEPS = TILES_PER_SAMPLE + 1


def _ret_scan_kernel(cdec_ref, qkv_ref, gate_ref, xl_ref, xc_ref, mod_ref, intra_ref, qdec_ref, kdec_ref,
                     wout_ref, gffn_ref, wrt_ref, x1_ref, h2_ref, aff_ref, s_ref, yf_ref):
    p = pl.program_id(1)
    j = pl.program_id(2)

    @pl.when(j == 0)
    def _():
        s_ref[...] = jnp.zeros_like(s_ref)

    def chunk_out(c):
        r0 = c * RET_CHUNK
        outs = []
        wide = lambda t: jnp.concatenate([t] * (DK // LANE), axis=1)
        for hd in range(HEADS):
            q = qkv_ref[r0:r0 + RET_CHUNK, hd * DK:(hd + 1) * DK]
            k = qkv_ref[r0:r0 + RET_CHUNK, D + hd * DK:D + (hd + 1) * DK]
            v = qkv_ref[r0:r0 + RET_CHUNK, 2 * D + hd * DV:2 * D + (hd + 1) * DV]
            att = lax.dot_general(q, k, (((1,), (1,)), ((), ())), preferred_element_type=F32)
            att = (att * intra_ref[hd]).astype(BF16)
            s_old = s_ref[hd]
            o = jnp.dot(att, v, preferred_element_type=F32)
            o = o + jnp.dot(q, s_old.astype(BF16), preferred_element_type=F32) * wide(qdec_ref[hd])
            ks = (k.astype(F32) * wide(kdec_ref[hd])).astype(BF16)
            kv = lax.dot_general(ks, v, (((0,), (0,)), ((), ())), preferred_element_type=F32)
            s_ref[hd] = cdec_ref[p, hd] * s_old + kv
            mu = jnp.mean(o, axis=-1, keepdims=True)
            oc = o - mu
            var = jnp.mean(oc * oc, axis=-1, keepdims=True)
            gate = gate_ref[r0:r0 + RET_CHUNK, hd * DV:(hd + 1) * DV].astype(F32)
            outs.append(gate * (oc * lax.rsqrt(var + EPS)))
        return jnp.concatenate(outs, axis=1)

    @pl.when(p == 0)
    def _():
        row = pl.multiple_of(j * TM, TM)
        for c in range(TM // RET_CHUNK):
            yf_ref[pl.ds(row + c * RET_CHUNK, RET_CHUNK), :] = chunk_out(c)

    @pl.when(p == 1)
    def _():
        pos = jnp.where(j == 0, 0, SCAN_STEPS - j)
        row = pl.multiple_of(pos * TM, TM)
        ys = [None] * (TM // RET_CHUNK)
        for c in reversed(range(TM // RET_CHUNK)):
            ys[c] = chunk_out(c) + yf_ref[pl.ds(row + c * RET_CHUNK, RET_CHUNK), :]
        y = jnp.concatenate(ys, axis=0).astype(BF16)
        y = jnp.dot(y, wout_ref[...], preferred_element_type=F32)
        x = jnp.where(j == 0, xc_ref[...], xl_ref[...])
        x1, h2, aff = _post(x, y, mod_ref[...], gffn_ref[...], wrt_ref[...])
        x1_ref[...] = x1
        h2_ref[...] = h2
        aff_ref[...] = aff


def _scan_tile(b, p, j):
    lat = b * TILES_PER_SAMPLE + jnp.where(p == 0, j - 1, TILES_PER_SAMPLE - j)
    return jnp.where(j == 0, LAT_TILES + b, lat)


def _ret_scan(qkv, gates, x_lat, x_ctx, mods_l, cdec, intra, qdec, kdec, wout, gffn, wrt):
    full = lambda shape: pl.BlockSpec(shape, lambda b, p, j, *_: (0,) * len(shape))
    tile = lambda b, p, j, *_: (_scan_tile(b, p, j), 0)
    tile_p1 = lambda b, p, j, *_: (_scan_tile(b, 1, jnp.where(p == 0, 0, j)), 0)
    tile_p1_t = lambda b, p, j, *_: (0, _scan_tile(b, 1, jnp.where(p == 0, 0, j)))
    lat_p1 = lambda b, p, j, *_: (b * TILES_PER_SAMPLE + TILES_PER_SAMPLE - jnp.where(p == 0, 1, jnp.maximum(j, 1)), 0)
    per_dir = lambda shape: pl.BlockSpec((None,) + shape, lambda b, p, j, *_: (p,) + (0,) * len(shape))
    grid_spec = pltpu.PrefetchScalarGridSpec(
        num_scalar_prefetch=0,
        grid=(BATCH, 2, SCAN_STEPS),
        in_specs=[
            pl.BlockSpec(memory_space=pltpu.SMEM),
            pl.BlockSpec((TM, 3 * D), tile),
            pl.BlockSpec((TM, D), lambda b, p, j, *_: (_scan_tile(b, p, j), p)),
            pl.BlockSpec((TM, D), lat_p1),
            pl.BlockSpec((TM, D), lambda b, p, j, *_: (b, 0)),
            pl.BlockSpec((None, 1, N_MOD * D), lambda b, p, j, *_: (jnp.where(j == 0, CTX_GROUP, b), 0, 0)),
            per_dir((HEADS, RET_CHUNK, RET_CHUNK)), per_dir((HEADS, RET_CHUNK, LANE)),
            per_dir((HEADS, RET_CHUNK, LANE)),
            full((D, D)), full((1, D)), full((N_EXPERTS, D)),
        ],
        out_specs=[
            pl.BlockSpec((TM, D), tile_p1),
            pl.BlockSpec((TM, D), tile_p1),
            pl.BlockSpec((N_EXPERTS, TM), tile_p1_t),
        ],
        scratch_shapes=[
            pltpu.VMEM((HEADS, DK, DV), F32),
            pltpu.VMEM((SCAN_STEPS * TM, D), F32),
        ],
    )
    return pl.pallas_call(
        _ret_scan_kernel,
        grid_spec=grid_spec,
        out_shape=[
            jax.ShapeDtypeStruct((TOK, D), F32),
            jax.ShapeDtypeStruct((TOK, D), BF16),
            jax.ShapeDtypeStruct((N_EXPERTS, TOK), F32),
        ],
        compiler_params=_cparams(("arbitrary", "arbitrary", "arbitrary")),
        name="retention_scan",
    )(cdec, qkv, gates, x_lat, x_ctx, mods_l, intra, qdec, kdec, wout, gffn, wrt)


def _decay_tables(decay_f, decay_b):
    idx = jnp.arange(RET_CHUNK, dtype=F32)
    diff = idx[:, None] - idx[None, :]
    lg_f = jax.nn.log_sigmoid(decay_f.astype(F32))
    lg_b = jax.nn.log_sigmoid(decay_b.astype(F32))
    mask_f = diff >= 0
    intra_f = jnp.where(mask_f, jnp.exp(jnp.where(mask_f, diff, 0.0)[None] * lg_f[:, None, None]), 0.0)
    mask_b = diff < 0
    intra_b = jnp.where(mask_b, jnp.exp(jnp.where(mask_b, -diff, 0.0)[None] * lg_b[:, None, None]), 0.0)
    qdec_f = jnp.exp((idx + 1.0)[None] * lg_f[:, None])
    kdec_f = jnp.exp((RET_CHUNK - 1.0 - idx)[None] * lg_f[:, None])
    qdec_b = jnp.exp((RET_CHUNK - idx)[None] * lg_b[:, None])
    kdec_b = jnp.exp(idx[None] * lg_b[:, None])
    wide = lambda t: jnp.broadcast_to(t[:, :, None], (HEADS, RET_CHUNK, LANE))
    intra = jnp.stack([intra_f, intra_b])
    qdec = jnp.stack([wide(qdec_f), wide(qdec_b)])
    kdec = jnp.stack([wide(kdec_f), wide(kdec_b)])
    cdec = jnp.stack([jnp.exp(RET_CHUNK * lg_f), jnp.exp(RET_CHUNK * lg_b)])
    return cdec, intra, qdec, kdec


def _rope_tables():
    rows = SEQ // GRID_W
    pos_r = np.repeat(np.arange(rows), GRID_W).astype(np.float64)
    pos_c = np.tile(np.arange(GRID_W), rows).astype(np.float64)
    n_freq = DK // 4
    inv = np.power(ROPE_BASE, -np.arange(n_freq, dtype=np.float64) / n_freq)
    ang = np.concatenate([pos_r[:, None] * inv[None], pos_c[:, None] * inv[None]], axis=-1)
    cos = np.concatenate([np.cos(ang), np.ones((TMD, DK // 2))], axis=0)
    sin = np.concatenate([np.sin(ang), np.zeros((TMD, DK // 2))], axis=0)
    return jnp.asarray(cos, F32), jnp.asarray(sin, F32)


KEY_ONE = 0x3F800000
BISECT_STEPS = 31
REFINE_STEPS = 12


def _route_kernel(aff_ref, w2_ref, pos_ref, offs_ref, *, n, cap):
    sample = lambda ref, s: ref[:, s * n:(s + 1) * n]
    count = lambda m: jnp.sum(jnp.where(m, 1.0, 0.0), axis=1, keepdims=True)

    def bisect(_, carry):
        nxt = []
        for s in range(BATCH):
            lo, hi = carry[s]
            mid = lo + ((hi - lo + 1) >> 1)
            ok = count(sample(aff_ref, s) >= pltpu.bitcast(mid, F32)) >= cap
            nxt.append((jnp.where(ok, mid, lo), jnp.where(ok, hi, mid - 1)))
        return tuple(nxt)

    lo0 = jnp.zeros((N_EXPERTS, 1), jnp.int32)
    hi0 = jnp.full((N_EXPERTS, 1), KEY_ONE, jnp.int32)
    keys = lax.fori_loop(0, BISECT_STEPS, bisect, ((lo0, hi0),) * BATCH)

    def refine(_, carry):
        nxt = []
        for s in range(BATCH):
            a, b = carry[s]
            m = a + (b - a) * 0.5
            ok = count(sample(aff_ref, s) >= m) >= cap
            nxt.append((jnp.where(ok, m, a), jnp.where(ok, b, m)))
        return tuple(nxt)

    cuts = lax.fori_loop(0, REFINE_STEPS, refine,
                         tuple((pltpu.bitcast(k, F32), pltpu.bitcast(k + 1, F32)) for k, _ in keys))

    w2 = w2_ref[...]
    lane = lax.broadcasted_iota(jnp.int32, (N_EXPERTS, LANE), 1)

    nb = n // LANE
    sblk = min(cap, LANE)

    def prefix(mask):
        carry = jnp.zeros((N_EXPERTS, LANE), F32)
        offs = jnp.zeros((N_EXPERTS, LANE), F32)
        first = [jnp.zeros((N_EXPERTS, LANE), F32) for _ in range(cap // sblk)]
        incs = []
        for j in range(nb):
            blk = mask[:, j * LANE:(j + 1) * LANE].astype(BF16)
            r = jnp.dot(blk, w2, preferred_element_type=F32)
            incs.append(r[:, :LANE] + carry)
            offs = jnp.where(lane == j, carry, offs)
            carry = carry + r[:, LANE:]
            first = [f + jnp.where(carry <= k * sblk, 1.0, 0.0) for k, f in enumerate(first)]
        offs = jnp.where(lane == nb, carry, offs)
        for k, f in enumerate(first):
            offs = jnp.where(lane == nb + 1 + k, f, offs)
        return jnp.concatenate(incs, axis=1), offs

    for s in range(BATCH):
        aff = sample(aff_ref, s)
        thr, above = cuts[s]
        gt = jnp.where(aff >= above, 1.0, 0.0)
        eq = jnp.where(aff >= thr, 1.0, 0.0) - gt
        need = cap - jnp.sum(gt, axis=1, keepdims=True)
        tie_inc, _ = prefix(eq)
        sel = gt + eq * jnp.where(tie_inc <= need, 1.0, 0.0)
        inc, offs = prefix(sel)
        pos_ref[:, s * n:(s + 1) * n] = jnp.where(sel > 0.0, inc.astype(jnp.int32) - 1, -1)
        offs_ref[s] = offs.astype(jnp.int32)


def _route(aff_t, w2, n, col0):
    cap = EC_CAPACITY * n // N_EXPERTS
    return pl.pallas_call(
        functools.partial(_route_kernel, n=n, cap=cap),
        grid=(1,),
        in_specs=[pl.BlockSpec((N_EXPERTS, BATCH * n), lambda i: (0, col0 // (BATCH * n))),
                  pl.BlockSpec((LANE, 2 * LANE), lambda i: (0, 0))],
        out_specs=[pl.BlockSpec((N_EXPERTS, BATCH * n), lambda i: (0, 0)),
                   pl.BlockSpec((BATCH, N_EXPERTS, LANE), lambda i: (0, 0, 0))],
        out_shape=[jax.ShapeDtypeStruct((N_EXPERTS, BATCH * n), jnp.int32),
                   jax.ShapeDtypeStruct((BATCH, N_EXPERTS, LANE), jnp.int32)],
        compiler_params=_cparams(("arbitrary",)),
        name="route_%d" % n,
    )(aff_t, w2)


GATHER_WINDOW = 1280
GATHER_FFN_VMEM = 60 * 1024 * 1024


def _gather_ffn_kernel(offs_ref, pos_ref, h2_ref, wg_ref, wu_ref, wd_ref, ye_ref, xs_ref, wbf_ref,
                       *, n, cap, spb):
    e = pl.program_id(0)
    b0 = pl.program_id(1) * spb

    @pl.when(pl.program_id(1) == 0)
    def _():
        for i, w_ref in enumerate((wg_ref, wu_ref, wd_ref)):
            wbf_ref[i] = w_ref[...].astype(BF16)

    sblk = min(cap, LANE)
    nb = n // LANE
    wb = min(GATHER_WINDOW, n) // LANE
    for s in range(spb):
        base = ((b0 + s) * N_EXPERTS + e) * LANE
        for sb in range(cap // sblk):
            s0 = sb * sblk
            rows = slice(s * cap + s0, s * cap + s0 + sblk)
            slot = s0 + lax.broadcasted_iota(jnp.int32, (sblk, LANE), 0)

            def window(first_blk, lo_slot):
                start = jnp.minimum(first_blk, nb - wb) if wb < nb else 0
                pieces = []
                for i in range(wb):
                    row = pos_ref[e, pl.ds(s * nb + start + i, 1), :]
                    row = jnp.where(row >= lo_slot, row, -1)
                    pieces.append(jnp.where(row == slot, 1.0, 0.0).astype(BF16))
                onehot = jnp.concatenate(pieces, axis=1)
                tok0 = pl.multiple_of((s * nb + start) * LANE, LANE)
                got = jnp.dot(onehot, h2_ref[pl.ds(tok0, wb * LANE), :], preferred_element_type=F32)
                return got.astype(BF16), start + wb

            got, nxt = window(offs_ref[base + nb + 1 + sb], s0)
            xs_ref[rows, :] = got
            if wb < nb:
                def more(nxt):
                    return jnp.logical_and(nxt < nb, offs_ref[base + jnp.minimum(nxt, nb)] < s0 + sblk)

                def extra(nxt):
                    got, nxt2 = window(nxt, offs_ref[base + nxt])
                    xs_ref[rows, :] = xs_ref[rows, :] + got
                    return nxt2

                lax.while_loop(more, extra, nxt)
    xs = xs_ref[...]
    a = jnp.dot(xs, wbf_ref[0], preferred_element_type=F32)
    u = jnp.dot(xs, wbf_ref[1], preferred_element_type=F32)
    hid = (_silu(a) * u).astype(BF16)
    ye_ref[...] = jnp.dot(hid, wbf_ref[2], preferred_element_type=F32).astype(BF16)


def _gather_ffn(offs, pos, h2, row0, layer, wg, wu, wd, n, spb):
    cap = EC_CAPACITY * n // N_EXPERTS
    wspec = pl.BlockSpec((None, None, D, D), lambda e, b, *_: (layer, e, 0, 0))
    grid_spec = pltpu.PrefetchScalarGridSpec(
        num_scalar_prefetch=1,
        grid=(N_EXPERTS, BATCH // spb),
        in_specs=[
            pl.BlockSpec((N_EXPERTS, spb * n // LANE, LANE), lambda e, b, *_: (0, b, 0)),
            pl.BlockSpec((spb * n, D), lambda e, b, *_: (b + row0 // (spb * n), 0)),
            wspec, wspec, wspec,
        ],
        out_specs=pl.BlockSpec((None, spb * cap, D), lambda e, b, *_: (e, b, 0)),
        scratch_shapes=[pltpu.VMEM((spb * cap, D), BF16), pltpu.VMEM((3, D, D), BF16)],
    )
    return pl.pallas_call(
        functools.partial(_gather_ffn_kernel, n=n, cap=cap, spb=spb),
        grid_spec=grid_spec,
        out_shape=jax.ShapeDtypeStruct((N_EXPERTS, BATCH * cap, D), BF16),
        compiler_params=_cparams(("arbitrary", "arbitrary"), GATHER_FFN_VMEM),
        name="gather_ffn_%d" % n,
    )(offs, pos, h2, wg, wu, wd)


STOK = 128
SCATTER_TILES = 2
SCATTER_COLS = 256


def _scatter_kernel(offs_ref, pos_ref, aff_ref, ye_ref, x1_ref, mod_ref, fg_ref, out_ref, pgt_ref,
                    *, n, cap, final):
    b = pl.program_id(0)
    tiles = n // STOK
    win = min(cap, 2 * STOK)
    shift = STOK.bit_length() - 1

    def window_start(e, t):
        if win == cap:
            return 0
        o0 = offs_ref[(b * N_EXPERTS + e) * LANE + t]
        return pl.multiple_of(jnp.minimum((o0 >> shift) << shift, cap - win), STOK)

    for u in range(SCATTER_TILES):
        t = pl.program_id(1) * SCATTER_TILES + u
        prow = t if tiles % 8 == 0 else b * tiles + t
        rows = slice(u * STOK, (u + 1) * STOK)
        for e in range(N_EXPERTS):
            row = pos_ref[e, pl.ds(prow, 1), :]
            gate = aff_ref[e, pl.ds(prow, 1), :]
            slot = window_start(e, t) + lax.broadcasted_iota(jnp.int32, (win, STOK), 0)
            pg = jnp.where(row == slot, gate, 0.0)
            pgt_ref[u, e] = pg.T.astype(BF16) if win >= LANE else pg.astype(BF16)
        for c in range(D // SCATTER_COLS):
            cols = slice(c * SCATTER_COLS, (c + 1) * SCATTER_COLS)
            acc = jnp.zeros((STOK, SCATTER_COLS), F32)
            for e in range(N_EXPERTS):
                yb = ye_ref[e, pl.ds(window_start(e, t), win), cols]
                if win >= LANE:
                    acc = acc + jnp.dot(pgt_ref[u, e], yb, preferred_element_type=F32)
                else:
                    acc = acc + lax.dot_general(pgt_ref[u, e], yb, (((0,), (0,)), ((), ())),
                                                preferred_element_type=F32)
            g2 = mod_ref[:, 5 * D + c * SCATTER_COLS:5 * D + (c + 1) * SCATTER_COLS]
            out_ref[rows, cols] = x1_ref[rows, cols] + g2 * acc
    if final:
        out_ref[...] = _rms(out_ref[...], fg_ref[...])


def _scatter(offs, pos, aff, ye, x1, mods_l, fg, n, row0, group_of, final=False):
    cap = EC_CAPACITY * n // N_EXPERTS
    win = min(cap, 2 * STOK)
    tiles = n // STOK
    steps = tiles // SCATTER_TILES
    rows = SCATTER_TILES * STOK
    blk0 = row0 // rows
    if tiles % 8 == 0:
        prows, pidx = tiles, lambda b: b
    else:
        prows, pidx = BATCH * tiles, lambda b: 0
    pgt_shape = (STOK, win) if win >= LANE else (win, STOK)
    grid_spec = pltpu.PrefetchScalarGridSpec(
        num_scalar_prefetch=1,
        grid=(BATCH, steps),
        in_specs=[
            pl.BlockSpec((N_EXPERTS, prows, STOK), lambda b, t, *_: (0, pidx(b), 0)),
            pl.BlockSpec((N_EXPERTS, prows, STOK), lambda b, t, *_: (0, pidx(b) + (row0 // STOK) // prows, 0)),
            pl.BlockSpec((N_EXPERTS, cap, D), lambda b, t, *_: (0, b, 0)),
            pl.BlockSpec((rows, D), lambda b, t, *_: (blk0 + b * steps + t, 0)),
            pl.BlockSpec((None, 1, N_MOD * D), lambda b, t, *_: (group_of(b), 0, 0)),
            pl.BlockSpec((1, D), lambda b, t, *_: (0, 0)),
        ],
        out_specs=pl.BlockSpec((rows, D), lambda b, t, *_: (b * steps + t, 0)),
        scratch_shapes=[pltpu.VMEM((SCATTER_TILES, N_EXPERTS) + pgt_shape, BF16)],
    )
    return pl.pallas_call(
        functools.partial(_scatter_kernel, n=n, cap=cap, final=final),
        grid_spec=grid_spec,
        out_shape=jax.ShapeDtypeStruct((BATCH * n, D), F32),
        compiler_params=_cparams(("arbitrary", "arbitrary")),
        name="scatter_%d" % n,
    )(offs, pos, aff, ye, x1, mods_l, fg)


def _prefix_weights():
    k = jnp.arange(LANE)
    upper = (k[:, None] <= k[None, :]).astype(BF16)
    return jnp.concatenate([upper, jnp.ones((LANE, LANE), BF16)], axis=1)


def kernel(x, c, ctx, c_ctx, ada_w, ada_b, norm_mix_g, norm_ffn_g, a_w_in, a_ln_g, a_ln_b, a_w_s, a_b_s,
           a_w_out, r_w_in, r_decay_f, r_decay_b, r_w_out, moe_w_router, moe_w_gate, moe_w_up, moe_w_down,
           final_norm_g):
    x_lat = x.reshape(N_LAT, D)
    x_ctx = ctx.reshape(N_CTX, D)
    cc = jnp.concatenate([c, c_ctx[None], jnp.zeros((MOD_ROWS - BATCH - 1, D), F32)], axis=0)
    mods = _mods(cc, ada_w, ada_b).reshape(DEPTH, MOD_ROWS, 1, N_MOD * D)
    cos_t, sin_t = _rope_tables()
    w2 = _prefix_weights()
    row = lambda v: v.reshape(1, -1)
    fg = row(final_norm_g)
    wg, wu, wd = moe_w_gate, moe_w_up, moe_w_down

    for i in range(DEPTH):
        last = i == DEPTH - 1
        j = i // 2
        gffn = row(norm_ffn_g[i])
        wrt = moe_w_router[i].T.astype(BF16)
        if i % 2 == 0:
            bs_full = jnp.repeat(a_b_s[j].T, CHUNK, axis=1)
            x1, h2, aff_t = _gmlp_layer(
                x_lat, x_ctx, mods[i], row(norm_mix_g[i]), a_w_in[j].astype(BF16), row(a_ln_g[j]),
                row(a_ln_b[j]), a_w_s[j].astype(BF16), bs_full, a_w_out[j].astype(BF16), gffn, wrt)
        else:
            qkv, gates = _ret_proj(x_lat, x_ctx, mods[i], row(norm_mix_g[i]), r_w_in[j].astype(BF16),
                                   cos_t, sin_t)
            cdec, intra, qdec, kdec = _decay_tables(r_decay_f[j], r_decay_b[j])
            x1, h2, aff_t = _ret_scan(qkv, gates, x_lat, x_ctx, mods[i], cdec, intra, qdec, kdec,
                                      r_w_out[j].astype(BF16), gffn, wrt)
        aff3 = aff_t.reshape(N_EXPERTS, TOK // STOK, STOK)

        pos, offs = _route(aff_t, w2, SEQ, 0)
        offs = offs.reshape(-1)
        pos = pos.reshape(N_EXPERTS, N_LAT // LANE, LANE)
        ye = _gather_ffn(offs, pos, h2, 0, i, wg, wu, wd, SEQ, 1)
        x_lat = _scatter(offs, pos, aff3, ye, x1, mods[i], fg, SEQ, 0, lambda b: b, final=last)
        if last:
            return x_lat.reshape(BATCH, SEQ, D)
        pos, offs = _route(aff_t, w2, CTX_LEN, N_LAT)
        offs = offs.reshape(-1)
        pos = pos.reshape(N_EXPERTS, N_CTX // LANE, LANE)
        ye = _gather_ffn(offs, pos, h2, N_LAT, i, wg, wu, wd, CTX_LEN, BATCH)
        x_ctx = _scatter(offs, pos, aff3, ye, x1, mods[i], fg, CTX_LEN, N_LAT, lambda b: CTX_GROUP)
```

```python
import functools

import jax
import jax.numpy as jnp
import numpy as np
from jax import lax
from jax.experimental import pallas as pl
from jax.experimental.pallas import tpu as pltpu

D = 1024
BATCH = 8
SEQ = 4096
DEPTH = 4
GRID_W = 64
CTX_LEN = 256
N_MOD = 6
CHUNK = 128
RET_CHUNK = 256
LANE = 128
A_GROUPS = 8
HEADS = 4
DK = 256
DV = 256
ROPE_BASE = 10000.0
N_EXPERTS = 16
EC_CAPACITY = 2
EPS = 1e-6

N_LAT = BATCH * SEQ
N_CTX = BATCH * CTX_LEN
TOK = N_LAT + N_CTX
TM = 256
LAT_TILES = N_LAT // TM
TILES_PER_SAMPLE = SEQ // TM
TMD = 512
LAT_DTILES = N_LAT // TMD
DTILES = TOK // TMD
DTILES_PER_SAMPLE = SEQ // TMD
CTX_GROUP = BATCH
MOD_ROWS = 16

F32 = jnp.float32
BF16 = jnp.bfloat16
VMEM_LIMIT = 48 * 1024 * 1024


def _cparams(sem, vmem=VMEM_LIMIT):
    return pltpu.CompilerParams(dimension_semantics=sem, vmem_limit_bytes=vmem)


def _rms(x, g):
    return x * lax.rsqrt(jnp.mean(x * x, axis=-1, keepdims=True) + EPS) * g


def _silu(x):
    return x * (1.0 / (1.0 + jnp.exp(-x)))


def _gelu_tanh(x):
    return 0.5 * x * (1.0 + jnp.tanh(0.7978845608028654 * (x + 0.044715 * (x * x * x))))


def _post(x, y, mod, gffn, wrt):
    g1 = mod[:, 2 * D:3 * D]
    sh2 = mod[:, 3 * D:4 * D]
    sc2 = mod[:, 4 * D:5 * D]
    x1 = x + g1 * y
    h2 = (_rms(x1, gffn) * (1.0 + sc2) + sh2).astype(BF16)
    logits = lax.dot_general(wrt, h2, (((1,), (1,)), ((), ())), preferred_element_type=F32)
    m = jnp.max(logits, axis=0, keepdims=True)
    e = jnp.exp(logits - m)
    aff = e / jnp.sum(e, axis=0, keepdims=True)
    return x1, h2, aff


def _mods_kernel(cc_ref, w_ref, b_ref, o_ref):
    a = _silu(cc_ref[...]).astype(BF16)
    o_ref[...] = jnp.dot(a, w_ref[...].astype(BF16), preferred_element_type=F32) + b_ref[...]


def _mods(cc, ada_w, ada_b):
    tn = 1536
    return pl.pallas_call(
        _mods_kernel,
        grid=(DEPTH, N_MOD * D // tn),
        in_specs=[
            pl.BlockSpec((MOD_ROWS, D), lambda i, n: (0, 0)),
            pl.BlockSpec((None, D, tn), lambda i, n: (i, 0, n)),
            pl.BlockSpec((None, 1, tn), lambda i, n: (i, 0, n)),
        ],
        out_specs=pl.BlockSpec((None, MOD_ROWS, tn), lambda i, n: (i, 0, n)),
        out_shape=jax.ShapeDtypeStruct((DEPTH, MOD_ROWS, N_MOD * D), F32),
        compiler_params=_cparams(("parallel", "parallel")),
        name="adaln_mods",
    )(cc, ada_w, ada_b.reshape(DEPTH, 1, N_MOD * D))


def _tile_group(i):
    return jnp.minimum(i // DTILES_PER_SAMPLE, CTX_GROUP)


def _gmlp_kernel(xl_ref, xc_ref, mod_ref, gmix_ref, win_ref, lng_ref, lnb_ref, ws_ref, bs_ref, wout_ref,
                 gffn_ref, wrt_ref, x1_ref, h2_ref, aff_ref):
    x = jnp.where(pl.program_id(0) < LAT_DTILES, xl_ref[...], xc_ref[...])
    mod = mod_ref[...]
    h = (_rms(x, gmix_ref[...]) * (1.0 + mod[:, D:2 * D]) + mod[:, 0:D]).astype(BF16)
    z = _gelu_tanh(jnp.dot(h, win_ref[...], preferred_element_type=F32))
    u = z[:, :D]
    v = z[:, D:]
    mu = jnp.mean(v, axis=-1, keepdims=True)
    vc = v - mu
    var = jnp.mean(vc * vc, axis=-1, keepdims=True)
    v = (vc * lax.rsqrt(var + EPS) * lng_ref[...] + lnb_ref[...]).astype(BF16)
    bs = bs_ref[...]
    rows = []
    for c in range(TMD // CHUNK):
        cols = []
        for g in range(A_GROUPS):
            vg = v[c * CHUNK:(c + 1) * CHUNK, g * CHUNK:(g + 1) * CHUNK]
            s = jnp.dot(ws_ref[g], vg, preferred_element_type=F32)
            cols.append(s + bs[:, g * CHUNK:(g + 1) * CHUNK])
        rows.append(jnp.concatenate(cols, axis=1))
    s = jnp.concatenate(rows, axis=0)
    y = jnp.dot((u * s).astype(BF16), wout_ref[...], preferred_element_type=F32)
    x1, h2, aff = _post(x, y, mod, gffn_ref[...], wrt_ref[...])
    x1_ref[...] = x1
    h2_ref[...] = h2
    aff_ref[...] = aff


def _gmlp_layer(x_lat, x_ctx, mods_l, gmix, win, lng, lnb, ws, bs_full, wout, gffn, wrt):
    full = lambda shape: pl.BlockSpec(shape, lambda i: (0,) * len(shape))
    return pl.pallas_call(
        _gmlp_kernel,
        grid=(DTILES,),
        in_specs=[
            pl.BlockSpec((TMD, D), lambda i: (jnp.minimum(i, LAT_DTILES - 1), 0)),
            pl.BlockSpec((TMD, D), lambda i: (jnp.maximum(i - LAT_DTILES, 0), 0)),
            pl.BlockSpec((None, 1, N_MOD * D), lambda i: (_tile_group(i), 0, 0)),
            full((1, D)), full((D, 2 * D)), full((1, D)), full((1, D)),
            full((A_GROUPS, CHUNK, CHUNK)), full((CHUNK, D)), full((D, D)),
            full((1, D)), full((N_EXPERTS, D)),
        ],
        out_specs=[
            pl.BlockSpec((TMD, D), lambda i: (i, 0)),
            pl.BlockSpec((TMD, D), lambda i: (i, 0)),
            pl.BlockSpec((N_EXPERTS, TMD), lambda i: (0, i)),
        ],
        out_shape=[
            jax.ShapeDtypeStruct((TOK, D), F32),
            jax.ShapeDtypeStruct((TOK, D), BF16),
            jax.ShapeDtypeStruct((N_EXPERTS, TOK), F32),
        ],
        compiler_params=_cparams(("parallel",)),
        name="gmlp_mixer",
    )(x_lat, x_ctx, mods_l, gmix, win, lng, lnb, ws, bs_full, wout, gffn, wrt)


def _ret_proj_kernel(xl_ref, xc_ref, mod_ref, gmix_ref, win_ref, cos_ref, sin_ref, qkv_ref, gate_ref):
    x = jnp.where(pl.program_id(0) < LAT_DTILES, xl_ref[...], xc_ref[...])
    mod = mod_ref[...]
    h = (_rms(x, gmix_ref[...]) * (1.0 + mod[:, D:2 * D]) + mod[:, 0:D]).astype(BF16)
    z = jnp.dot(h, win_ref[...], preferred_element_type=F32)
    cos = cos_ref[...]
    sin = sin_ref[...]
    half = DK // 2
    for part, scale in ((0, 1.0), (1, DK ** -0.5)):
        for hd in range(HEADS):
            base = part * D + hd * DK
            t1 = z[:, base:base + half]
            t2 = z[:, base + half:base + DK]
            qkv_ref[:, base:base + half] = ((t1 * cos - t2 * sin) * scale).astype(BF16)
            qkv_ref[:, base + half:base + DK] = ((t1 * sin + t2 * cos) * scale).astype(BF16)
    qkv_ref[:, 2 * D:3 * D] = z[:, 2 * D:3 * D].astype(BF16)
    gate_ref[...] = _silu(z[:, 3 * D:5 * D]).astype(BF16)


def _ret_proj(x_lat, x_ctx, mods_l, gmix, win, cos_t, sin_t):
    full = lambda shape: pl.BlockSpec(shape, lambda i: (0,) * len(shape))
    rope_blk = lambda i: (jnp.where(i < LAT_DTILES, i % DTILES_PER_SAMPLE, DTILES_PER_SAMPLE), 0)
    return pl.pallas_call(
        _ret_proj_kernel,
        grid=(DTILES,),
        in_specs=[
            pl.BlockSpec((TMD, D), lambda i: (jnp.minimum(i, LAT_DTILES - 1), 0)),
            pl.BlockSpec((TMD, D), lambda i: (jnp.maximum(i - LAT_DTILES, 0), 0)),
            pl.BlockSpec((None, 1, N_MOD * D), lambda i: (_tile_group(i), 0, 0)),
            full((1, D)), full((D, 5 * D)),
            pl.BlockSpec((TMD, DK // 2), rope_blk),
            pl.BlockSpec((TMD, DK // 2), rope_blk),
        ],
        out_specs=[
            pl.BlockSpec((TMD, 3 * D), lambda i: (i, 0)),
            pl.BlockSpec((TMD, 2 * D), lambda i: (i, 0)),
        ],
        out_shape=[
            jax.ShapeDtypeStruct((TOK, 3 * D), BF16),
            jax.ShapeDtypeStruct((TOK, 2 * D), BF16),
        ],
        compiler_params=_cparams(("parallel",)),
        name="retention_proj",
    )(x_lat, x_ctx, mods_l, gmix, win, cos_t, sin_t)


SCAN_STEPS = TILES_PER_SAMPLE + 1


def _ret_scan_kernel(cdec_ref, qkv_ref, gate_ref, xl_ref, xc_ref, mod_ref, intra_ref, qdec_ref, kdec_ref,
                     wout_ref, gffn_ref, wrt_ref, x1_ref, h2_ref, aff_ref, s_ref, yf_ref):
    p = pl.program_id(1)
    j = pl.program_id(2)

    @pl.when(j == 0)
    def _():
        s_ref[...] = jnp.zeros_like(s_ref)

    def chunk_out(c):
        r0 = c * RET_CHUNK
        outs = []
        wide = lambda t: jnp.concatenate([t] * (DK // LANE), axis=1)
        for hd in range(HEADS):
            q = qkv_ref[r0:r0 + RET_CHUNK, hd * DK:(hd + 1) * DK]
            k = qkv_ref[r0:r0 + RET_CHUNK, D + hd * DK:D + (hd + 1) * DK]
            v = qkv_ref[r0:r0 + RET_CHUNK, 2 * D + hd * DV:2 * D + (hd + 1) * DV]
            att = lax.dot_general(q, k, (((1,), (1,)), ((), ())), preferred_element_type=F32)
            att = (att * intra_ref[hd]).astype(BF16)
            s_old = s_ref[hd]
            o = jnp.dot(att, v, preferred_element_type=F32)
            o = o + jnp.dot(q, s_old.astype(BF16), preferred_element_type=F32) * wide(qdec_ref[hd])
            ks = (k.astype(F32) * wide(kdec_ref[hd])).astype(BF16)
            kv = lax.dot_general(ks, v, (((0,), (0,)), ((), ())), preferred_element_type=F32)
            s_ref[hd] = cdec_ref[p, hd] * s_old + kv
            mu = jnp.mean(o, axis=-1, keepdims=True)
            oc = o - mu
            var = jnp.mean(oc * oc, axis=-1, keepdims=True)
            gate = gate_ref[r0:r0 + RET_CHUNK, hd * DV:(hd + 1) * DV].astype(F32)
            outs.append(gate * (oc * lax.rsqrt(var + EPS)))
        return jnp.concatenate(outs, axis=1)

    @pl.when(p == 0)
    def _():
        row = pl.multiple_of(j * TM, TM)
        for c in range(TM // RET_CHUNK):
            yf_ref[pl.ds(row + c * RET_CHUNK, RET_CHUNK), :] = chunk_out(c)

    @pl.when(p == 1)
    def _():
        pos = jnp.where(j == 0, 0, SCAN_STEPS - j)
        row = pl.multiple_of(pos * TM, TM)
        ys = [None] * (TM // RET_CHUNK)
        for c in reversed(range(TM // RET_CHUNK)):
            ys[c] = chunk_out(c) + yf_ref[pl.ds(row + c * RET_CHUNK, RET_CHUNK), :]
        y = jnp.concatenate(ys, axis=0).astype(BF16)
        y = jnp.dot(y, wout_ref[...], preferred_element_type=F32)
        x = jnp.where(j == 0, xc_ref[...], xl_ref[...])
        x1, h2, aff = _post(x, y, mod_ref[...], gffn_ref[...], wrt_ref[...])
        x1_ref[...] = x1
        h2_ref[...] = h2
        aff_ref[...] = aff


def _scan_tile(b, p, j):
    lat = b * TILES_PER_SAMPLE + jnp.where(p == 0, j - 1, TILES_PER_SAMPLE - j)
    return jnp.where(j == 0, LAT_TILES + b, lat)


def _ret_scan(qkv, gates, x_lat, x_ctx, mods_l, cdec, intra, qdec, kdec, wout, gffn, wrt):
    full = lambda shape: pl.BlockSpec(shape, lambda b, p, j, *_: (0,) * len(shape))
    tile = lambda b, p, j, *_: (_scan_tile(b, p, j), 0)
    tile_p1 = lambda b, p, j, *_: (_scan_tile(b, 1, jnp.where(p == 0, 0, j)), 0)
    tile_p1_t = lambda b, p, j, *_: (0, _scan_tile(b, 1, jnp.where(p == 0, 0, j)))
    lat_p1 = lambda b, p, j, *_: (b * TILES_PER_SAMPLE + TILES_PER_SAMPLE - jnp.where(p == 0, 1, jnp.maximum(j, 1)), 0)
    per_dir = lambda shape: pl.BlockSpec((None,) + shape, lambda b, p, j, *_: (p,) + (0,) * len(shape))
    grid_spec = pltpu.PrefetchScalarGridSpec(
        num_scalar_prefetch=0,
        grid=(BATCH, 2, SCAN_STEPS),
        in_specs=[
            pl.BlockSpec(memory_space=pltpu.SMEM),
            pl.BlockSpec((TM, 3 * D), tile),
            pl.BlockSpec((TM, D), lambda b, p, j, *_: (_scan_tile(b, p, j), p)),
            pl.BlockSpec((TM, D), lat_p1),
            pl.BlockSpec((TM, D), lambda b, p, j, *_: (b, 0)),
            pl.BlockSpec((None, 1, N_MOD * D), lambda b, p, j, *_: (jnp.where(j == 0, CTX_GROUP, b), 0, 0)),
            per_dir((HEADS, RET_CHUNK, RET_CHUNK)), per_dir((HEADS, RET_CHUNK, LANE)),
            per_dir((HEADS, RET_CHUNK, LANE)),
            full((D, D)), full((1, D)), full((N_EXPERTS, D)),
        ],
        out_specs=[
            pl.BlockSpec((TM, D), tile_p1),
            pl.BlockSpec((TM, D), tile_p1),
            pl.BlockSpec((N_EXPERTS, TM), tile_p1_t),
        ],
        scratch_shapes=[
            pltpu.VMEM((HEADS, DK, DV), F32),
            pltpu.VMEM((SCAN_STEPS * TM, D), F32),
        ],
    )
    return pl.pallas_call(
        _ret_scan_kernel,
        grid_spec=grid_spec,
        out_shape=[
            jax.ShapeDtypeStruct((TOK, D), F32),
            jax.ShapeDtypeStruct((TOK, D), BF16),
            jax.ShapeDtypeStruct((N_EXPERTS, TOK), F32),
        ],
        compiler_params=_cparams(("arbitrary", "arbitrary", "arbitrary")),
        name="retention_scan",
    )(cdec, qkv, gates, x_lat, x_ctx, mods_l, intra, qdec, kdec, wout, gffn, wrt)


def _decay_tables(decay_f, decay_b):
    idx = jnp.arange(RET_CHUNK, dtype=F32)
    diff = idx[:, None] - idx[None, :]
    lg_f = jax.nn.log_sigmoid(decay_f.astype(F32))
    lg_b = jax.nn.log_sigmoid(decay_b.astype(F32))
    mask_f = diff >= 0
    intra_f = jnp.where(mask_f, jnp.exp(jnp.where(mask_f, diff, 0.0)[None] * lg_f[:, None, None]), 0.0)
    mask_b = diff < 0
    intra_b = jnp.where(mask_b, jnp.exp(jnp.where(mask_b, -diff, 0.0)[None] * lg_b[:, None, None]), 0.0)
    qdec_f = jnp.exp((idx + 1.0)[None] * lg_f[:, None])
    kdec_f = jnp.exp((RET_CHUNK - 1.0 - idx)[None] * lg_f[:, None])
    qdec_b = jnp.exp((RET_CHUNK - idx)[None] * lg_b[:, None])
    kdec_b = jnp.exp(idx[None] * lg_b[:, None])
    wide = lambda t: jnp.broadcast_to(t[:, :, None], (HEADS, RET_CHUNK, LANE))
    intra = jnp.stack([intra_f, intra_b])
    qdec = jnp.stack([wide(qdec_f), wide(qdec_b)])
    kdec = jnp.stack([wide(kdec_f), wide(kdec_b)])
    cdec = jnp.stack([jnp.exp(RET_CHUNK * lg_f), jnp.exp(RET_CHUNK * lg_b)])
    return cdec, intra, qdec, kdec


def _rope_tables():
    rows = SEQ // GRID_W
    pos_r = np.repeat(np.arange(rows), GRID_W).astype(np.float64)
    pos_c = np.tile(np.arange(GRID_W), rows).astype(np.float64)
    n_freq = DK // 4
    inv = np.power(ROPE_BASE, -np.arange(n_freq, dtype=np.float64) / n_freq)
    ang = np.concatenate([pos_r[:, None] * inv[None], pos_c[:, None] * inv[None]], axis=-1)
    cos = np.concatenate([np.cos(ang), np.ones((TMD, DK // 2))], axis=0)
    sin = np.concatenate([np.sin(ang), np.zeros((TMD, DK // 2))], axis=0)
    return jnp.asarray(cos, F32), jnp.asarray(sin, F32)


KEY_ONE = 0x3F800000
BISECT_STEPS = 31
REFINE_STEPS = 12


def _route_kernel(aff_ref, w2_ref, pos_ref, offs_ref, *, n, cap):
    sample = lambda ref, s: ref[:, s * n:(s + 1) * n]
    count = lambda m: jnp.sum(jnp.where(m, 1.0, 0.0), axis=1, keepdims=True)

    def bisect(_, carry):
        nxt = []
        for s in range(BATCH):
            lo, hi = carry[s]
            mid = lo + ((hi - lo + 1) >> 1)
            ok = count(sample(aff_ref, s) >= pltpu.bitcast(mid, F32)) >= cap
            nxt.append((jnp.where(ok, mid, lo), jnp.where(ok, hi, mid - 1)))
        return tuple(nxt)

    lo0 = jnp.zeros((N_EXPERTS, 1), jnp.int32)
    hi0 = jnp.full((N_EXPERTS, 1), KEY_ONE, jnp.int32)
    keys = lax.fori_loop(0, BISECT_STEPS, bisect, ((lo0, hi0),) * BATCH)

    def refine(_, carry):
        nxt = []
        for s in range(BATCH):
            a, b = carry[s]
            m = a + (b - a) * 0.5
            ok = count(sample(aff_ref, s) >= m) >= cap
            nxt.append((jnp.where(ok, m, a), jnp.where(ok, b, m)))
        return tuple(nxt)

    cuts = lax.fori_loop(0, REFINE_STEPS, refine,
                         tuple((pltpu.bitcast(k, F32), pltpu.bitcast(k + 1, F32)) for k, _ in keys))

    w2 = w2_ref[...]
    lane = lax.broadcasted_iota(jnp.int32, (N_EXPERTS, LANE), 1)

    nb = n // LANE
    sblk = min(cap, LANE)

    def prefix(mask):
        carry = jnp.zeros((N_EXPERTS, LANE), F32)
        offs = jnp.zeros((N_EXPERTS, LANE), F32)
        first = [jnp.zeros((N_EXPERTS, LANE), F32) for _ in range(cap // sblk)]
        incs = []
        for j in range(nb):
            blk = mask[:, j * LANE:(j + 1) * LANE].astype(BF16)
            r = jnp.dot(blk, w2, preferred_element_type=F32)
            incs.append(r[:, :LANE] + carry)
            offs = jnp.where(lane == j, carry, offs)
            carry = carry + r[:, LANE:]
            first = [f + jnp.where(carry <= k * sblk, 1.0, 0.0) for k, f in enumerate(first)]
        offs = jnp.where(lane == nb, carry, offs)
        for k, f in enumerate(first):
            offs = jnp.where(lane == nb + 1 + k, f, offs)
        return jnp.concatenate(incs, axis=1), offs

    for s in range(BATCH):
        aff = sample(aff_ref, s)
        thr, above = cuts[s]
        gt = jnp.where(aff >= above, 1.0, 0.0)
        eq = jnp.where(aff >= thr, 1.0, 0.0) - gt
        need = cap - jnp.sum(gt, axis=1, keepdims=True)
        tie_inc, _ = prefix(eq)
        sel = gt + eq * jnp.where(tie_inc <= need, 1.0, 0.0)
        inc, offs = prefix(sel)
        pos_ref[:, s * n:(s + 1) * n] = jnp.where(sel > 0.0, inc.astype(jnp.int32) - 1, -1)
        offs_ref[s] = offs.astype(jnp.int32)


def _route(aff_t, w2, n, col0):
    cap = EC_CAPACITY * n // N_EXPERTS
    return pl.pallas_call(
        functools.partial(_route_kernel, n=n, cap=cap),
        grid=(1,),
        in_specs=[pl.BlockSpec((N_EXPERTS, BATCH * n), lambda i: (0, col0 // (BATCH * n))),
                  pl.BlockSpec((LANE, 2 * LANE), lambda i: (0, 0))],
        out_specs=[pl.BlockSpec((N_EXPERTS, BATCH * n), lambda i: (0, 0)),
                   pl.BlockSpec((BATCH, N_EXPERTS, LANE), lambda i: (0, 0, 0))],
        out_shape=[jax.ShapeDtypeStruct((N_EXPERTS, BATCH * n), jnp.int32),
                   jax.ShapeDtypeStruct((BATCH, N_EXPERTS, LANE), jnp.int32)],
        compiler_params=_cparams(("arbitrary",)),
        name="route_%d" % n,
    )(aff_t, w2)


GATHER_WINDOW = 1280
GATHER_FFN_VMEM = 60 * 1024 * 1024


def _gather_ffn_kernel(offs_ref, pos_ref, h2_ref, wg_ref, wu_ref, wd_ref, ye_ref, xs_ref, wbf_ref,
                       *, n, cap, spb):
    e = pl.program_id(0)
    b0 = pl.program_id(1) * spb

    @pl.when(pl.program_id(1) == 0)
    def _():
        for i, w_ref in enumerate((wg_ref, wu_ref, wd_ref)):
            wbf_ref[i] = w_ref[...].astype(BF16)

    sblk = min(cap, LANE)
    nb = n // LANE
    wb = min(GATHER_WINDOW, n) // LANE
    for s in range(spb):
        base = ((b0 + s) * N_EXPERTS + e) * LANE
        for sb in range(cap // sblk):
            s0 = sb * sblk
            rows = slice(s * cap + s0, s * cap + s0 + sblk)
            slot = s0 + lax.broadcasted_iota(jnp.int32, (sblk, LANE), 0)

            def window(first_blk, lo_slot):
                start = jnp.minimum(first_blk, nb - wb) if wb < nb else 0
                pieces = []
                for i in range(wb):
                    row = pos_ref[e, pl.ds(s * nb + start + i, 1), :]
                    row = jnp.where(row >= lo_slot, row, -1)
                    pieces.append(jnp.where(row == slot, 1.0, 0.0).astype(BF16))
                onehot = jnp.concatenate(pieces, axis=1)
                tok0 = pl.multiple_of((s * nb + start) * LANE, LANE)
                got = jnp.dot(onehot, h2_ref[pl.ds(tok0, wb * LANE), :], preferred_element_type=F32)
                return got.astype(BF16), start + wb

            got, nxt = window(offs_ref[base + nb + 1 + sb], s0)
            xs_ref[rows, :] = got
            if wb < nb:
                def more(nxt):
                    return jnp.logical_and(nxt < nb, offs_ref[base + jnp.minimum(nxt, nb)] < s0 + sblk)

                def extra(nxt):
                    got, nxt2 = window(nxt, offs_ref[base + nxt])
                    xs_ref[rows, :] = xs_ref[rows, :] + got
                    return nxt2

                lax.while_loop(more, extra, nxt)
    xs = xs_ref[...]
    a = jnp.dot(xs, wbf_ref[0], preferred_element_type=F32)
    u = jnp.dot(xs, wbf_ref[1], preferred_element_type=F32)
    hid = (_silu(a) * u).astype(BF16)
    ye_ref[...] = jnp.dot(hid, wbf_ref[2], preferred_element_type=F32).astype(BF16)


def _gather_ffn(offs, pos, h2, row0, layer, wg, wu, wd, n, spb):
    cap = EC_CAPACITY * n // N_EXPERTS
    steps = BATCH // spb

    def wspec(lead):
        return pl.BlockSpec((None, None, D, D), lambda e, b, *_: (
            layer, jnp.minimum((e * steps + b + lead) // steps, N_EXPERTS - 1), 0, 0))

    leads = [(i * steps) // 3 for i in range(3)]
    grid_spec = pltpu.PrefetchScalarGridSpec(
        num_scalar_prefetch=1,
        grid=(N_EXPERTS, BATCH // spb),
        in_specs=[
            pl.BlockSpec((N_EXPERTS, spb * n // LANE, LANE), lambda e, b, *_: (0, b, 0)),
            pl.BlockSpec((spb * n, D), lambda e, b, *_: (b + row0 // (spb * n), 0)),
            wspec(leads[0]), wspec(leads[1]), wspec(leads[2]),
        ],
        out_specs=pl.BlockSpec((None, spb * cap, D), lambda e, b, *_: (e, b, 0)),
        scratch_shapes=[pltpu.VMEM((spb * cap, D), BF16), pltpu.VMEM((3, D, D), BF16)],
    )
    return pl.pallas_call(
        functools.partial(_gather_ffn_kernel, n=n, cap=cap, spb=spb),
        grid_spec=grid_spec,
        out_shape=jax.ShapeDtypeStruct((N_EXPERTS, BATCH * cap, D), BF16),
        compiler_params=_cparams(("arbitrary", "arbitrary"), GATHER_FFN_VMEM),
        name="gather_ffn_%d" % n,
    )(offs, pos, h2, wg, wu, wd)


STOK = 128
SCATTER_TILES = 4
SCATTER_COLS = 256


def _scatter_kernel(offs_ref, pos_ref, aff_ref, ye_ref, x1_ref, mod_ref, fg_ref, out_ref, pgt_ref,
                    *, n, cap, final):
    b = pl.program_id(0)
    tiles = n // STOK
    tiles_per_step = min(SCATTER_TILES, tiles)
    win = min(cap, 2 * STOK)
    shift = STOK.bit_length() - 1

    def window_start(e, t):
        if win == cap:
            return 0
        o0 = offs_ref[(b * N_EXPERTS + e) * LANE + t]
        return pl.multiple_of(jnp.minimum((o0 >> shift) << shift, cap - win), STOK)

    for u in range(tiles_per_step):
        t = pl.program_id(1) * tiles_per_step + u
        prow = t if tiles % 8 == 0 else b * tiles + t
        rows = slice(u * STOK, (u + 1) * STOK)
        for e in range(N_EXPERTS):
            row = pos_ref[e, pl.ds(prow, 1), :]
            gate = aff_ref[e, pl.ds(prow, 1), :]
            slot = window_start(e, t) + lax.broadcasted_iota(jnp.int32, (win, STOK), 0)
            pg = jnp.where(row == slot, gate, 0.0)
            pgt_ref[u, e] = pg.T.astype(BF16) if win >= LANE else pg.astype(BF16)
        for c in range(D // SCATTER_COLS):
            cols = slice(c * SCATTER_COLS, (c + 1) * SCATTER_COLS)
            acc = jnp.zeros((STOK, SCATTER_COLS), F32)
            for e in range(N_EXPERTS):
                yb = ye_ref[e, pl.ds(window_start(e, t), win), cols]
                if win >= LANE:
                    acc = acc + jnp.dot(pgt_ref[u, e], yb, preferred_element_type=F32)
                else:
                    acc = acc + lax.dot_general(pgt_ref[u, e], yb, (((0,), (0,)), ((), ())),
                                                preferred_element_type=F32)
            g2 = mod_ref[:, 5 * D + c * SCATTER_COLS:5 * D + (c + 1) * SCATTER_COLS]
            out_ref[rows, cols] = x1_ref[rows, cols] + g2 * acc
    if final:
        out_ref[...] = _rms(out_ref[...], fg_ref[...])


def _scatter(offs, pos, aff, ye, x1, mods_l, fg, n, row0, group_of, final=False):
    cap = EC_CAPACITY * n // N_EXPERTS
    win = min(cap, 2 * STOK)
    tiles = n // STOK
    tiles_per_step = min(SCATTER_TILES, tiles)
    steps = tiles // tiles_per_step
    rows = tiles_per_step * STOK
    blk0 = row0 // rows
    if tiles % 8 == 0:
        prows, pidx = tiles, lambda b: b
    else:
        prows, pidx = BATCH * tiles, lambda b: 0
    pgt_shape = (STOK, win) if win >= LANE else (win, STOK)
    grid_spec = pltpu.PrefetchScalarGridSpec(
        num_scalar_prefetch=1,
        grid=(BATCH, steps),
        in_specs=[
            pl.BlockSpec((N_EXPERTS, prows, STOK), lambda b, t, *_: (0, pidx(b), 0)),
            pl.BlockSpec((N_EXPERTS, prows, STOK), lambda b, t, *_: (0, pidx(b) + (row0 // STOK) // prows, 0)),
            pl.BlockSpec((N_EXPERTS, cap, D), lambda b, t, *_: (0, b, 0)),
            pl.BlockSpec((rows, D), lambda b, t, *_: (blk0 + b * steps + t, 0)),
            pl.BlockSpec((None, 1, N_MOD * D), lambda b, t, *_: (group_of(b), 0, 0)),
            pl.BlockSpec((1, D), lambda b, t, *_: (0, 0)),
        ],
        out_specs=pl.BlockSpec((rows, D), lambda b, t, *_: (b * steps + t, 0)),
        scratch_shapes=[pltpu.VMEM((tiles_per_step, N_EXPERTS) + pgt_shape, BF16)],
    )
    return pl.pallas_call(
        functools.partial(_scatter_kernel, n=n, cap=cap, final=final),
        grid_spec=grid_spec,
        out_shape=jax.ShapeDtypeStruct((BATCH * n, D), F32),
        compiler_params=_cparams(("arbitrary", "arbitrary")),
        name="scatter_%d" % n,
    )(offs, pos, aff, ye, x1, mods_l, fg)


def _prefix_weights():
    k = jnp.arange(LANE)
    upper = (k[:, None] <= k[None, :]).astype(BF16)
    return jnp.concatenate([upper, jnp.ones((LANE, LANE), BF16)], axis=1)


def kernel(x, c, ctx, c_ctx, ada_w, ada_b, norm_mix_g, norm_ffn_g, a_w_in, a_ln_g, a_ln_b, a_w_s, a_b_s,
           a_w_out, r_w_in, r_decay_f, r_decay_b, r_w_out, moe_w_router, moe_w_gate, moe_w_up, moe_w_down,
           final_norm_g):
    x_lat = x.reshape(N_LAT, D)
    x_ctx = ctx.reshape(N_CTX, D)
    cc = jnp.concatenate([c, c_ctx[None], jnp.zeros((MOD_ROWS - BATCH - 1, D), F32)], axis=0)
    mods = _mods(cc, ada_w, ada_b).reshape(DEPTH, MOD_ROWS, 1, N_MOD * D)
    cos_t, sin_t = _rope_tables()
    w2 = _prefix_weights()
    row = lambda v: v.reshape(1, -1)
    fg = row(final_norm_g)
    wg, wu, wd = moe_w_gate, moe_w_up, moe_w_down

    for i in range(DEPTH):
        last = i == DEPTH - 1
        j = i // 2
        gffn = row(norm_ffn_g[i])
        wrt = moe_w_router[i].T.astype(BF16)
        if i % 2 == 0:
            bs_full = jnp.repeat(a_b_s[j].T, CHUNK, axis=1)
            x1, h2, aff_t = _gmlp_layer(
                x_lat, x_ctx, mods[i], row(norm_mix_g[i]), a_w_in[j].astype(BF16), row(a_ln_g[j]),
                row(a_ln_b[j]), a_w_s[j].astype(BF16), bs_full, a_w_out[j].astype(BF16), gffn, wrt)
        else:
            qkv, gates = _ret_proj(x_lat, x_ctx, mods[i], row(norm_mix_g[i]), r_w_in[j].astype(BF16),
                                   cos_t, sin_t)
            cdec, intra, qdec, kdec = _decay_tables(r_decay_f[j], r_decay_b[j])
            x1, h2, aff_t = _ret_scan(qkv, gates, x_lat, x_ctx, mods[i], cdec, intra, qdec, kdec,
                                      r_w_out[j].astype(BF16), gffn, wrt)
        aff3 = aff_t.reshape(N_EXPERTS, TOK // STOK, STOK)

        pos, offs = _route(aff_t, w2, SEQ, 0)
        offs = offs.reshape(-1)
        pos = pos.reshape(N_EXPERTS, N_LAT // LANE, LANE)
        ye = _gather_ffn(offs, pos, h2, 0, i, wg, wu, wd, SEQ, 1)
        x_lat = _scatter(offs, pos, aff3, ye, x1, mods[i], fg, SEQ, 0, lambda b: b, final=last)
        if last:
            return x_lat.reshape(BATCH, SEQ, D)
        pos, offs = _route(aff_t, w2, CTX_LEN, N_LAT)
        offs = offs.reshape(-1)
        pos = pos.reshape(N_EXPERTS, N_CTX // LANE, LANE)
        ye = _gather_ffn(offs, pos, h2, N_LAT, i, wg, wu, wd, CTX_LEN, BATCH)
        x_ctx = _scatter(offs, pos, aff3, ye, x1, mods[i], fg, CTX_LEN, N_LAT, lambda b: CTX_GROUP)
```

```python
import functools

import jax
import jax.numpy as jnp
import numpy as np
from jax import lax
from jax.experimental import pallas as pl
from jax.experimental.pallas import tpu as pltpu

D = 1024
BATCH = 8
SEQ = 4096
DEPTH = 4
GRID_W = 64
CTX_LEN = 256
N_MOD = 6
CHUNK = 128
RET_CHUNK = 256
LANE = 128
A_GROUPS = 8
HEADS = 4
DK = 256
DV = 256
ROPE_BASE = 10000.0
N_EXPERTS = 16
EC_CAPACITY = 2
EPS = 1e-6

N_LAT = BATCH * SEQ
N_CTX = BATCH * CTX_LEN
TOK = N_LAT + N_CTX
TM = 256
LAT_TILES = N_LAT // TM
TILES_PER_SAMPLE = SEQ // TM
TMD = 512
LAT_DTILES = N_LAT // TMD
DTILES = TOK // TMD
DTILES_PER_SAMPLE = SEQ // TMD
CTX_GROUP = BATCH
MOD_ROWS = 16

F32 = jnp.float32
BF16 = jnp.bfloat16
VMEM_LIMIT = 48 * 1024 * 1024


def _cparams(sem, vmem=VMEM_LIMIT):
    return pltpu.CompilerParams(dimension_semantics=sem, vmem_limit_bytes=vmem)


def _rms(x, g):
    return x * lax.rsqrt(jnp.mean(x * x, axis=-1, keepdims=True) + EPS) * g


def _silu(x):
    return x * (1.0 / (1.0 + jnp.exp(-x)))


def _gelu_tanh(x):
    return 0.5 * x * (1.0 + jnp.tanh(0.7978845608028654 * (x + 0.044715 * (x * x * x))))


def _post(x, y, mod, gffn, wrt):
    g1 = mod[:, 2 * D:3 * D]
    sh2 = mod[:, 3 * D:4 * D]
    sc2 = mod[:, 4 * D:5 * D]
    x1 = x + g1 * y
    h2 = (_rms(x1, gffn) * (1.0 + sc2) + sh2).astype(BF16)
    logits = lax.dot_general(wrt, h2, (((1,), (1,)), ((), ())), preferred_element_type=F32)
    m = jnp.max(logits, axis=0, keepdims=True)
    e = jnp.exp(logits - m)
    aff = e / jnp.sum(e, axis=0, keepdims=True)
    return x1, h2, aff


def _mods_kernel(cc_ref, w_ref, b_ref, o_ref):
    a = _silu(cc_ref[...]).astype(BF16)
    o_ref[...] = jnp.dot(a, w_ref[...].astype(BF16), preferred_element_type=F32) + b_ref[...]


def _mods(cc, ada_w, ada_b):
    tn = 1536
    return pl.pallas_call(
        _mods_kernel,
        grid=(DEPTH, N_MOD * D // tn),
        in_specs=[
            pl.BlockSpec((MOD_ROWS, D), lambda i, n: (0, 0)),
            pl.BlockSpec((None, D, tn), lambda i, n: (i, 0, n)),
            pl.BlockSpec((None, 1, tn), lambda i, n: (i, 0, n)),
        ],
        out_specs=pl.BlockSpec((None, MOD_ROWS, tn), lambda i, n: (i, 0, n)),
        out_shape=jax.ShapeDtypeStruct((DEPTH, MOD_ROWS, N_MOD * D), F32),
        compiler_params=_cparams(("parallel", "parallel")),
        name="adaln_mods",
    )(cc, ada_w, ada_b.reshape(DEPTH, 1, N_MOD * D))


def _tile_group(i):
    return jnp.minimum(i // DTILES_PER_SAMPLE, CTX_GROUP)


def _gmlp_kernel(xl_ref, xc_ref, mod_ref, gmix_ref, win_ref, lng_ref, lnb_ref, ws_ref, bs_ref, wout_ref,
                 gffn_ref, wrt_ref, x1_ref, h2_ref, aff_ref):
    x = jnp.where(pl.program_id(0) < LAT_DTILES, xl_ref[...], xc_ref[...])
    mod = mod_ref[...]
    h = (_rms(x, gmix_ref[...]) * (1.0 + mod[:, D:2 * D]) + mod[:, 0:D]).astype(BF16)
    z = _gelu_tanh(jnp.dot(h, win_ref[...], preferred_element_type=F32))
    u = z[:, :D]
    v = z[:, D:]
    mu = jnp.mean(v, axis=-1, keepdims=True)
    vc = v - mu
    var = jnp.mean(vc * vc, axis=-1, keepdims=True)
    v = (vc * lax.rsqrt(var + EPS) * lng_ref[...] + lnb_ref[...]).astype(BF16)
    bs = bs_ref[...]
    rows = []
    for c in range(TMD // CHUNK):
        cols = []
        for g in range(A_GROUPS):
            vg = v[c * CHUNK:(c + 1) * CHUNK, g * CHUNK:(g + 1) * CHUNK]
            s = jnp.dot(ws_ref[g], vg, preferred_element_type=F32)
            cols.append(s + bs[:, g * CHUNK:(g + 1) * CHUNK])
        rows.append(jnp.concatenate(cols, axis=1))
    s = jnp.concatenate(rows, axis=0)
    y = jnp.dot((u * s).astype(BF16), wout_ref[...], preferred_element_type=F32)
    x1, h2, aff = _post(x, y, mod, gffn_ref[...], wrt_ref[...])
    x1_ref[...] = x1
    h2_ref[...] = h2
    aff_ref[...] = aff


def _gmlp_layer(x_lat, x_ctx, mods_l, gmix, win, lng, lnb, ws, bs_full, wout, gffn, wrt):
    full = lambda shape: pl.BlockSpec(shape, lambda i: (0,) * len(shape))
    return pl.pallas_call(
        _gmlp_kernel,
        grid=(DTILES,),
        in_specs=[
            pl.BlockSpec((TMD, D), lambda i: (jnp.minimum(i, LAT_DTILES - 1), 0)),
            pl.BlockSpec((TMD, D), lambda i: (jnp.maximum(i - LAT_DTILES, 0), 0)),
            pl.BlockSpec((None, 1, N_MOD * D), lambda i: (_tile_group(i), 0, 0)),
            full((1, D)), full((D, 2 * D)), full((1, D)), full((1, D)),
            full((A_GROUPS, CHUNK, CHUNK)), full((CHUNK, D)), full((D, D)),
            full((1, D)), full((N_EXPERTS, D)),
        ],
        out_specs=[
            pl.BlockSpec((TMD, D), lambda i: (i, 0)),
            pl.BlockSpec((TMD, D), lambda i: (i, 0)),
            pl.BlockSpec((N_EXPERTS, TMD), lambda i: (0, i)),
        ],
        out_shape=[
            jax.ShapeDtypeStruct((TOK, D), F32),
            jax.ShapeDtypeStruct((TOK, D), BF16),
            jax.ShapeDtypeStruct((N_EXPERTS, TOK), F32),
        ],
        compiler_params=_cparams(("parallel",)),
        name="gmlp_mixer",
    )(x_lat, x_ctx, mods_l, gmix, win, lng, lnb, ws, bs_full, wout, gffn, wrt)


def _ret_proj_kernel(xl_ref, xc_ref, mod_ref, gmix_ref, win_ref, wkt_ref, cos_ref, sin_ref, cost_ref, sint_ref,
                     qv_ref, kt_ref, gate_ref):
    x = jnp.where(pl.program_id(0) < LAT_DTILES, xl_ref[...], xc_ref[...])
    mod = mod_ref[...]
    h = (_rms(x, gmix_ref[...]) * (1.0 + mod[:, D:2 * D]) + mod[:, 0:D]).astype(BF16)
    z = jnp.dot(h, win_ref[...], preferred_element_type=F32)
    zkt = lax.dot_general(wkt_ref[...], h, (((1,), (1,)), ((), ())), preferred_element_type=F32)
    cos = cos_ref[...]
    sin = sin_ref[...]
    cost = cost_ref[...]
    sint = sint_ref[...]
    half = DK // 2
    kscale = DK ** -0.5
    for hd in range(HEADS):
        base = hd * DK
        t1 = z[:, base:base + half]
        t2 = z[:, base + half:base + DK]
        qv_ref[:, base:base + half] = (t1 * cos - t2 * sin).astype(BF16)
        qv_ref[:, base + half:base + DK] = (t1 * sin + t2 * cos).astype(BF16)
        k1 = zkt[base:base + half, :]
        k2 = zkt[base + half:base + DK, :]
        kt_ref[base:base + half, :] = ((k1 * cost - k2 * sint) * kscale).astype(BF16)
        kt_ref[base + half:base + DK, :] = ((k1 * sint + k2 * cost) * kscale).astype(BF16)
    qv_ref[:, D:2 * D] = z[:, D:2 * D].astype(BF16)
    gate_ref[...] = _silu(z[:, 2 * D:4 * D]).astype(BF16)


def _ret_proj(x_lat, x_ctx, mods_l, gmix, win, wkt, cos_t, sin_t, cos_tt, sin_tt):
    full = lambda shape: pl.BlockSpec(shape, lambda i: (0,) * len(shape))
    rope_i = lambda i: jnp.where(i < LAT_DTILES, i % DTILES_PER_SAMPLE, DTILES_PER_SAMPLE)
    return pl.pallas_call(
        _ret_proj_kernel,
        grid=(DTILES,),
        in_specs=[
            pl.BlockSpec((TMD, D), lambda i: (jnp.minimum(i, LAT_DTILES - 1), 0)),
            pl.BlockSpec((TMD, D), lambda i: (jnp.maximum(i - LAT_DTILES, 0), 0)),
            pl.BlockSpec((None, 1, N_MOD * D), lambda i: (_tile_group(i), 0, 0)),
            full((1, D)), full((D, 4 * D)), full((D, D)),
            pl.BlockSpec((TMD, DK // 2), lambda i: (rope_i(i), 0)),
            pl.BlockSpec((TMD, DK // 2), lambda i: (rope_i(i), 0)),
            pl.BlockSpec((DK // 2, TMD), lambda i: (0, rope_i(i))),
            pl.BlockSpec((DK // 2, TMD), lambda i: (0, rope_i(i))),
        ],
        out_specs=[
            pl.BlockSpec((TMD, 2 * D), lambda i: (i, 0)),
            pl.BlockSpec((D, TMD), lambda i: (0, i)),
            pl.BlockSpec((TMD, 2 * D), lambda i: (i, 0)),
        ],
        out_shape=[
            jax.ShapeDtypeStruct((TOK, 2 * D), BF16),
            jax.ShapeDtypeStruct((D, TOK), BF16),
            jax.ShapeDtypeStruct((TOK, 2 * D), BF16),
        ],
        compiler_params=_cparams(("parallel",)),
        name="retention_proj",
    )(x_lat, x_ctx, mods_l, gmix, win, wkt, cos_t, sin_t, cos_tt, sin_tt)


SCAN_STEPS = TILES_PER_SAMPLE + 1


def _ret_scan_kernel(cdec_ref, qv_ref, kt_ref, gate_ref, xl_ref, xc_ref, mod_ref, intra_ref, qdec_ref, kdec_ref,
                     wout_ref, gffn_ref, wrt_ref, x1_ref, h2_ref, aff_ref, s_ref, yf_ref):
    p = pl.program_id(1)
    j = pl.program_id(2)

    @pl.when(j == 0)
    def _():
        s_ref[...] = jnp.zeros_like(s_ref)

    def chunk_out(c):
        r0 = c * RET_CHUNK
        outs = []
        wide = lambda t: jnp.concatenate([t] * (DK // LANE), axis=1)
        for hd in range(HEADS):
            q = qv_ref[r0:r0 + RET_CHUNK, hd * DK:(hd + 1) * DK]
            kt = kt_ref[hd * DK:(hd + 1) * DK, r0:r0 + RET_CHUNK]
            v = qv_ref[r0:r0 + RET_CHUNK, D + hd * DV:D + (hd + 1) * DV]
            att = jnp.dot(q, kt, preferred_element_type=F32)
            att = (att * intra_ref[hd]).astype(BF16)
            s_old = s_ref[hd]
            o = jnp.dot(att, v, preferred_element_type=F32)
            o = o + jnp.dot(q, s_old.astype(BF16), preferred_element_type=F32) * wide(qdec_ref[hd])
            ks = (kt.astype(F32) * kdec_ref[hd]).astype(BF16)
            kv = jnp.dot(ks, v, preferred_element_type=F32)
            s_ref[hd] = cdec_ref[p, hd] * s_old + kv
            mu = jnp.mean(o, axis=-1, keepdims=True)
            oc = o - mu
            var = jnp.mean(oc * oc, axis=-1, keepdims=True)
            gate = gate_ref[r0:r0 + RET_CHUNK, hd * DV:(hd + 1) * DV].astype(F32)
            outs.append(gate * (oc * lax.rsqrt(var + EPS)))
        return jnp.concatenate(outs, axis=1)

    @pl.when(p == 0)
    def _():
        row = pl.multiple_of(j * TM, TM)
        for c in range(TM // RET_CHUNK):
            yf_ref[pl.ds(row + c * RET_CHUNK, RET_CHUNK), :] = chunk_out(c)

    @pl.when(p == 1)
    def _():
        pos = jnp.where(j == 0, 0, SCAN_STEPS - j)
        row = pl.multiple_of(pos * TM, TM)
        ys = [None] * (TM // RET_CHUNK)
        for c in reversed(range(TM // RET_CHUNK)):
            ys[c] = chunk_out(c) + yf_ref[pl.ds(row + c * RET_CHUNK, RET_CHUNK), :]
        y = jnp.concatenate(ys, axis=0).astype(BF16)
        y = jnp.dot(y, wout_ref[...], preferred_element_type=F32)
        x = jnp.where(j == 0, xc_ref[...], xl_ref[...])
        x1, h2, aff = _post(x, y, mod_ref[...], gffn_ref[...], wrt_ref[...])
        x1_ref[...] = x1
        h2_ref[...] = h2
        aff_ref[...] = aff


def _scan_tile(b, p, j):
    lat = b * TILES_PER_SAMPLE + jnp.where(p == 0, j - 1, TILES_PER_SAMPLE - j)
    return jnp.where(j == 0, LAT_TILES + b, lat)


def _ret_scan(qv, kt, gates, x_lat, x_ctx, mods_l, cdec, intra, qdec, kdec, wout, gffn, wrt):
    full = lambda shape: pl.BlockSpec(shape, lambda b, p, j, *_: (0,) * len(shape))
    tile = lambda b, p, j, *_: (_scan_tile(b, p, j), 0)
    tile_p1 = lambda b, p, j, *_: (_scan_tile(b, 1, jnp.where(p == 0, 0, j)), 0)
    tile_p1_t = lambda b, p, j, *_: (0, _scan_tile(b, 1, jnp.where(p == 0, 0, j)))
    lat_p1 = lambda b, p, j, *_: (b * TILES_PER_SAMPLE + TILES_PER_SAMPLE - jnp.where(p == 0, 1, jnp.maximum(j, 1)), 0)
    per_dir = lambda shape: pl.BlockSpec((None,) + shape, lambda b, p, j, *_: (p,) + (0,) * len(shape))
    grid_spec = pltpu.PrefetchScalarGridSpec(
        num_scalar_prefetch=0,
        grid=(BATCH, 2, SCAN_STEPS),
        in_specs=[
            pl.BlockSpec(memory_space=pltpu.SMEM),
            pl.BlockSpec((TM, 2 * D), tile),
            pl.BlockSpec((D, TM), lambda b, p, j, *_: (0, _scan_tile(b, p, j))),
            pl.BlockSpec((TM, D), lambda b, p, j, *_: (_scan_tile(b, p, j), p)),
            pl.BlockSpec((TM, D), lat_p1),
            pl.BlockSpec((TM, D), lambda b, p, j, *_: (b, 0)),
            pl.BlockSpec((None, 1, N_MOD * D), lambda b, p, j, *_: (jnp.where(j == 0, CTX_GROUP, b), 0, 0)),
            per_dir((HEADS, RET_CHUNK, RET_CHUNK)), per_dir((HEADS, RET_CHUNK, LANE)),
            per_dir((HEADS, 1, RET_CHUNK)),
            full((D, D)), full((1, D)), full((N_EXPERTS, D)),
        ],
        out_specs=[
            pl.BlockSpec((TM, D), tile_p1),
            pl.BlockSpec((TM, D), tile_p1),
            pl.BlockSpec((N_EXPERTS, TM), tile_p1_t),
        ],
        scratch_shapes=[
            pltpu.VMEM((HEADS, DK, DV), F32),
            pltpu.VMEM((SCAN_STEPS * TM, D), F32),
        ],
    )
    return pl.pallas_call(
        _ret_scan_kernel,
        grid_spec=grid_spec,
        out_shape=[
            jax.ShapeDtypeStruct((TOK, D), F32),
            jax.ShapeDtypeStruct((TOK, D), BF16),
            jax.ShapeDtypeStruct((N_EXPERTS, TOK), F32),
        ],
        compiler_params=_cparams(("arbitrary", "arbitrary", "arbitrary")),
        name="retention_scan",
    )(cdec, qv, kt, gates, x_lat, x_ctx, mods_l, intra, qdec, kdec, wout, gffn, wrt)


def _decay_tables(decay_f, decay_b):
    idx = jnp.arange(RET_CHUNK, dtype=F32)
    diff = idx[:, None] - idx[None, :]
    lg_f = jax.nn.log_sigmoid(decay_f.astype(F32))
    lg_b = jax.nn.log_sigmoid(decay_b.astype(F32))
    mask_f = diff >= 0
    intra_f = jnp.where(mask_f, jnp.exp(jnp.where(mask_f, diff, 0.0)[None] * lg_f[:, None, None]), 0.0)
    mask_b = diff < 0
    intra_b = jnp.where(mask_b, jnp.exp(jnp.where(mask_b, -diff, 0.0)[None] * lg_b[:, None, None]), 0.0)
    qdec_f = jnp.exp((idx + 1.0)[None] * lg_f[:, None])
    kdec_f = jnp.exp((RET_CHUNK - 1.0 - idx)[None] * lg_f[:, None])
    qdec_b = jnp.exp((RET_CHUNK - idx)[None] * lg_b[:, None])
    kdec_b = jnp.exp(idx[None] * lg_b[:, None])
    wide = lambda t: jnp.broadcast_to(t[:, :, None], (HEADS, RET_CHUNK, LANE))
    intra = jnp.stack([intra_f, intra_b])
    qdec = jnp.stack([wide(qdec_f), wide(qdec_b)])
    kdec = jnp.stack([kdec_f, kdec_b])[:, :, None, :]
    cdec = jnp.stack([jnp.exp(RET_CHUNK * lg_f), jnp.exp(RET_CHUNK * lg_b)])
    return cdec, intra, qdec, kdec


def _rope_tables():
    rows = SEQ // GRID_W
    pos_r = np.repeat(np.arange(rows), GRID_W).astype(np.float64)
    pos_c = np.tile(np.arange(GRID_W), rows).astype(np.float64)
    n_freq = DK // 4
    inv = np.power(ROPE_BASE, -np.arange(n_freq, dtype=np.float64) / n_freq)
    ang = np.concatenate([pos_r[:, None] * inv[None], pos_c[:, None] * inv[None]], axis=-1)
    cos = np.concatenate([np.cos(ang), np.ones((TMD, DK // 2))], axis=0)
    sin = np.concatenate([np.sin(ang), np.zeros((TMD, DK // 2))], axis=0)
    return jnp.asarray(cos, F32), jnp.asarray(sin, F32), jnp.asarray(cos.T, F32), jnp.asarray(sin.T, F32)


KEY_ONE = 0x3F800000
BISECT_STEPS = 31
REFINE_STEPS = 12


def _route_kernel(aff_ref, w2_ref, pos_ref, offs_ref, *, n, cap):
    sample = lambda ref, s: ref[:, s * n:(s + 1) * n]
    count = lambda m: jnp.sum(jnp.where(m, 1.0, 0.0), axis=1, keepdims=True)

    def bisect(_, carry):
        nxt = []
        for s in range(BATCH):
            lo, hi = carry[s]
            mid = lo + ((hi - lo + 1) >> 1)
            ok = count(sample(aff_ref, s) >= pltpu.bitcast(mid, F32)) >= cap
            nxt.append((jnp.where(ok, mid, lo), jnp.where(ok, hi, mid - 1)))
        return tuple(nxt)

    lo0 = jnp.zeros((N_EXPERTS, 1), jnp.int32)
    hi0 = jnp.full((N_EXPERTS, 1), KEY_ONE, jnp.int32)
    keys = lax.fori_loop(0, BISECT_STEPS, bisect, ((lo0, hi0),) * BATCH)

    def refine(_, carry):
        nxt = []
        for s in range(BATCH):
            a, b = carry[s]
            m = a + (b - a) * 0.5
            ok = count(sample(aff_ref, s) >= m) >= cap
            nxt.append((jnp.where(ok, m, a), jnp.where(ok, b, m)))
        return tuple(nxt)

    cuts = lax.fori_loop(0, REFINE_STEPS, refine,
                         tuple((pltpu.bitcast(k, F32), pltpu.bitcast(k + 1, F32)) for k, _ in keys))

    w2 = w2_ref[...]
    lane = lax.broadcasted_iota(jnp.int32, (N_EXPERTS, LANE), 1)

    nb = n // LANE
    sblk = min(cap, LANE)

    def prefix(mask):
        carry = jnp.zeros((N_EXPERTS, LANE), F32)
        offs = jnp.zeros((N_EXPERTS, LANE), F32)
        first = [jnp.zeros((N_EXPERTS, LANE), F32) for _ in range(cap // sblk)]
        incs = []
        for j in range(nb):
            blk = mask[:, j * LANE:(j + 1) * LANE].astype(BF16)
            r = jnp.dot(blk, w2, preferred_element_type=F32)
            incs.append(r[:, :LANE] + carry)
            offs = jnp.where(lane == j, carry, offs)
            carry = carry + r[:, LANE:]
            first = [f + jnp.where(carry <= k * sblk, 1.0, 0.0) for k, f in enumerate(first)]
        offs = jnp.where(lane == nb, carry, offs)
        for k, f in enumerate(first):
            offs = jnp.where(lane == nb + 1 + k, f, offs)
        return jnp.concatenate(incs, axis=1), offs

    for s in range(BATCH):
        aff = sample(aff_ref, s)
        thr, above = cuts[s]
        gt = jnp.where(aff >= above, 1.0, 0.0)
        eq = jnp.where(aff >= thr, 1.0, 0.0) - gt
        need = cap - jnp.sum(gt, axis=1, keepdims=True)
        tie_inc, _ = prefix(eq)
        sel = gt + eq * jnp.where(tie_inc <= need, 1.0, 0.0)
        inc, offs = prefix(sel)
        pos_ref[:, s * n:(s + 1) * n] = jnp.where(sel > 0.0, inc.astype(jnp.int32) - 1, -1)
        offs_ref[s] = offs.astype(jnp.int32)


def _route(aff_t, w2, n, col0):
    cap = EC_CAPACITY * n // N_EXPERTS
    return pl.pallas_call(
        functools.partial(_route_kernel, n=n, cap=cap),
        grid=(1,),
        in_specs=[pl.BlockSpec((N_EXPERTS, BATCH * n), lambda i: (0, col0 // (BATCH * n))),
                  pl.BlockSpec((LANE, 2 * LANE), lambda i: (0, 0))],
        out_specs=[pl.BlockSpec((N_EXPERTS, BATCH * n), lambda i: (0, 0)),
                   pl.BlockSpec((BATCH, N_EXPERTS, LANE), lambda i: (0, 0, 0))],
        out_shape=[jax.ShapeDtypeStruct((N_EXPERTS, BATCH * n), jnp.int32),
                   jax.ShapeDtypeStruct((BATCH, N_EXPERTS, LANE), jnp.int32)],
        compiler_params=_cparams(("arbitrary",)),
        name="route_%d" % n,
    )(aff_t, w2)


GATHER_WINDOW = 1280
GATHER_FFN_VMEM = 60 * 1024 * 1024


def _gather_ffn_kernel(offs_ref, pos_ref, h2_ref, wg_ref, wu_ref, wd_ref, ye_ref, xs_ref, wbf_ref,
                       *, n, cap, spb):
    e = pl.program_id(0)
    b0 = pl.program_id(1) * spb

    @pl.when(pl.program_id(1) == 0)
    def _():
        for i, w_ref in enumerate((wg_ref, wu_ref, wd_ref)):
            wbf_ref[i] = w_ref[...].astype(BF16)

    sblk = min(cap, LANE)
    nb = n // LANE
    wb = min(GATHER_WINDOW, n) // LANE
    for s in range(spb):
        base = ((b0 + s) * N_EXPERTS + e) * LANE
        for sb in range(cap // sblk):
            s0 = sb * sblk
            rows = slice(s * cap + s0, s * cap + s0 + sblk)
            slot = s0 + lax.broadcasted_iota(jnp.int32, (sblk, LANE), 0)

            def window(first_blk, lo_slot):
                start = jnp.minimum(first_blk, nb - wb) if wb < nb else 0
                pieces = []
                for i in range(wb):
                    row = pos_ref[e, pl.ds(s * nb + start + i, 1), :]
                    row = jnp.where(row >= lo_slot, row, -1)
                    pieces.append(jnp.where(row == slot, 1.0, 0.0).astype(BF16))
                onehot = jnp.concatenate(pieces, axis=1)
                tok0 = pl.multiple_of((s * nb + start) * LANE, LANE)
                got = jnp.dot(onehot, h2_ref[pl.ds(tok0, wb * LANE), :], preferred_element_type=F32)
                return got.astype(BF16), start + wb

            got, nxt = window(offs_ref[base + nb + 1 + sb], s0)
            xs_ref[rows, :] = got
            if wb < nb:
                def more(nxt):
                    return jnp.logical_and(nxt < nb, offs_ref[base + jnp.minimum(nxt, nb)] < s0 + sblk)

                def extra(nxt):
                    got, nxt2 = window(nxt, offs_ref[base + nxt])
                    xs_ref[rows, :] = xs_ref[rows, :] + got
                    return nxt2

                lax.while_loop(more, extra, nxt)
    xs = xs_ref[...]
    a = jnp.dot(xs, wbf_ref[0], preferred_element_type=F32)
    u = jnp.dot(xs, wbf_ref[1], preferred_element_type=F32)
    hid = (_silu(a) * u).astype(BF16)
    ye_ref[...] = jnp.dot(hid, wbf_ref[2], preferred_element_type=F32).astype(BF16)


def _gather_ffn(offs, pos, h2, row0, layer, wg, wu, wd, n, spb):
    cap = EC_CAPACITY * n // N_EXPERTS
    steps = BATCH // spb

    def wspec(lead):
        return pl.BlockSpec((None, None, D, D), lambda e, b, *_: (
            layer, jnp.minimum((e * steps + b + lead) // steps, N_EXPERTS - 1), 0, 0))

    leads = [(i * steps) // 3 for i in range(3)]
    grid_spec = pltpu.PrefetchScalarGridSpec(
        num_scalar_prefetch=1,
        grid=(N_EXPERTS, BATCH // spb),
        in_specs=[
            pl.BlockSpec((N_EXPERTS, spb * n // LANE, LANE), lambda e, b, *_: (0, b, 0)),
            pl.BlockSpec((spb * n, D), lambda e, b, *_: (b + row0 // (spb * n), 0)),
            wspec(leads[0]), wspec(leads[1]), wspec(leads[2]),
        ],
        out_specs=pl.BlockSpec((None, spb * cap, D), lambda e, b, *_: (e, b, 0)),
        scratch_shapes=[pltpu.VMEM((spb * cap, D), BF16), pltpu.VMEM((3, D, D), BF16)],
    )
    return pl.pallas_call(
        functools.partial(_gather_ffn_kernel, n=n, cap=cap, spb=spb),
        grid_spec=grid_spec,
        out_shape=jax.ShapeDtypeStruct((N_EXPERTS, BATCH * cap, D), BF16),
        compiler_params=_cparams(("arbitrary", "arbitrary"), GATHER_FFN_VMEM),
        name="gather_ffn_%d" % n,
    )(offs, pos, h2, wg, wu, wd)


STOK = 128
SCATTER_TILES = 4
SCATTER_COLS = 256


def _scatter_kernel(offs_ref, pos_ref, aff_ref, ye_ref, x1_ref, mod_ref, fg_ref, out_ref, pgt_ref,
                    *, n, cap, final):
    b = pl.program_id(0)
    tiles = n // STOK
    tiles_per_step = min(SCATTER_TILES, tiles)
    win = min(cap, 2 * STOK)
    shift = STOK.bit_length() - 1

    def window_start(e, t):
        if win == cap:
            return 0
        o0 = offs_ref[(b * N_EXPERTS + e) * LANE + t]
        return pl.multiple_of(jnp.minimum((o0 >> shift) << shift, cap - win), STOK)

    for u in range(tiles_per_step):
        t = pl.program_id(1) * tiles_per_step + u
        prow = t if tiles % 8 == 0 else b * tiles + t
        rows = slice(u * STOK, (u + 1) * STOK)
        for e in range(N_EXPERTS):
            row = pos_ref[e, pl.ds(prow, 1), :]
            gate = aff_ref[e, pl.ds(prow, 1), :]
            slot = window_start(e, t) + lax.broadcasted_iota(jnp.int32, (win, STOK), 0)
            pg = jnp.where(row == slot, gate, 0.0)
            pgt_ref[u, e] = pg.T.astype(BF16) if win >= LANE else pg.astype(BF16)
        for c in range(D // SCATTER_COLS):
            cols = slice(c * SCATTER_COLS, (c + 1) * SCATTER_COLS)
            acc = jnp.zeros((STOK, SCATTER_COLS), F32)
            for e in range(N_EXPERTS):
                yb = ye_ref[e, pl.ds(window_start(e, t), win), cols]
                if win >= LANE:
                    acc = acc + jnp.dot(pgt_ref[u, e], yb, preferred_element_type=F32)
                else:
                    acc = acc + lax.dot_general(pgt_ref[u, e], yb, (((0,), (0,)), ((), ())),
                                                preferred_element_type=F32)
            g2 = mod_ref[:, 5 * D + c * SCATTER_COLS:5 * D + (c + 1) * SCATTER_COLS]
            out_ref[rows, cols] = x1_ref[rows, cols] + g2 * acc
    if final:
        out_ref[...] = _rms(out_ref[...], fg_ref[...])


def _scatter(offs, pos, aff, ye, x1, mods_l, fg, n, row0, group_of, final=False):
    cap = EC_CAPACITY * n // N_EXPERTS
    win = min(cap, 2 * STOK)
    tiles = n // STOK
    tiles_per_step = min(SCATTER_TILES, tiles)
    steps = tiles // tiles_per_step
    rows = tiles_per_step * STOK
    blk0 = row0 // rows
    if tiles % 8 == 0:
        prows, pidx = tiles, lambda b: b
    else:
        prows, pidx = BATCH * tiles, lambda b: 0
    pgt_shape = (STOK, win) if win >= LANE else (win, STOK)
    grid_spec = pltpu.PrefetchScalarGridSpec(
        num_scalar_prefetch=1,
        grid=(BATCH, steps),
        in_specs=[
            pl.BlockSpec((N_EXPERTS, prows, STOK), lambda b, t, *_: (0, pidx(b), 0)),
            pl.BlockSpec((N_EXPERTS, prows, STOK), lambda b, t, *_: (0, pidx(b) + (row0 // STOK) // prows, 0)),
            pl.BlockSpec((N_EXPERTS, cap, D), lambda b, t, *_: (0, b, 0)),
            pl.BlockSpec((rows, D), lambda b, t, *_: (blk0 + b * steps + t, 0)),
            pl.BlockSpec((None, 1, N_MOD * D), lambda b, t, *_: (group_of(b), 0, 0)),
            pl.BlockSpec((1, D), lambda b, t, *_: (0, 0)),
        ],
        out_specs=pl.BlockSpec((rows, D), lambda b, t, *_: (b * steps + t, 0)),
        scratch_shapes=[pltpu.VMEM((tiles_per_step, N_EXPERTS) + pgt_shape, BF16)],
    )
    return pl.pallas_call(
        functools.partial(_scatter_kernel, n=n, cap=cap, final=final),
        grid_spec=grid_spec,
        out_shape=jax.ShapeDtypeStruct((BATCH * n, D), F32),
        compiler_params=_cparams(("arbitrary", "arbitrary")),
        name="scatter_%d" % n,
    )(offs, pos, aff, ye, x1, mods_l, fg)


def _prefix_weights():
    k = jnp.arange(LANE)
    upper = (k[:, None] <= k[None, :]).astype(BF16)
    return jnp.concatenate([upper, jnp.ones((LANE, LANE), BF16)], axis=1)


def kernel(x, c, ctx, c_ctx, ada_w, ada_b, norm_mix_g, norm_ffn_g, a_w_in, a_ln_g, a_ln_b, a_w_s, a_b_s,
           a_w_out, r_w_in, r_decay_f, r_decay_b, r_w_out, moe_w_router, moe_w_gate, moe_w_up, moe_w_down,
           final_norm_g):
    x_lat = x.reshape(N_LAT, D)
    x_ctx = ctx.reshape(N_CTX, D)
    cc = jnp.concatenate([c, c_ctx[None], jnp.zeros((MOD_ROWS - BATCH - 1, D), F32)], axis=0)
    mods = _mods(cc, ada_w, ada_b).reshape(DEPTH, MOD_ROWS, 1, N_MOD * D)
    cos_t, sin_t, cos_tt, sin_tt = _rope_tables()
    w2 = _prefix_weights()
    row = lambda v: v.reshape(1, -1)
    fg = row(final_norm_g)
    wg, wu, wd = moe_w_gate, moe_w_up, moe_w_down

    for i in range(DEPTH):
        last = i == DEPTH - 1
        j = i // 2
        gffn = row(norm_ffn_g[i])
        wrt = moe_w_router[i].T.astype(BF16)
        if i % 2 == 0:
            bs_full = jnp.repeat(a_b_s[j].T, CHUNK, axis=1)
            x1, h2, aff_t = _gmlp_layer(
                x_lat, x_ctx, mods[i], row(norm_mix_g[i]), a_w_in[j].astype(BF16), row(a_ln_g[j]),
                row(a_ln_b[j]), a_w_s[j].astype(BF16), bs_full, a_w_out[j].astype(BF16), gffn, wrt)
        else:
            w_in = r_w_in[j].astype(BF16)
            w_rest = jnp.concatenate([w_in[:, :D], w_in[:, 2 * D:]], axis=1)
            qv, kt, gates = _ret_proj(x_lat, x_ctx, mods[i], row(norm_mix_g[i]), w_rest, w_in[:, D:2 * D].T,
                                      cos_t, sin_t, cos_tt, sin_tt)
            cdec, intra, qdec, kdec = _decay_tables(r_decay_f[j], r_decay_b[j])
            x1, h2, aff_t = _ret_scan(qv, kt, gates, x_lat, x_ctx, mods[i], cdec, intra, qdec, kdec,
                                      r_w_out[j].astype(BF16), gffn, wrt)
        aff3 = aff_t.reshape(N_EXPERTS, TOK // STOK, STOK)

        pos, offs = _route(aff_t, w2, SEQ, 0)
        offs = offs.reshape(-1)
        pos = pos.reshape(N_EXPERTS, N_LAT // LANE, LANE)
        ye = _gather_ffn(offs, pos, h2, 0, i, wg, wu, wd, SEQ, 1)
        x_lat = _scatter(offs, pos, aff3, ye, x1, mods[i], fg, SEQ, 0, lambda b: b, final=last)
        if last:
            return x_lat.reshape(BATCH, SEQ, D)
        pos, offs = _route(aff_t, w2, CTX_LEN, N_LAT)
        offs = offs.reshape(-1)
        pos = pos.reshape(N_EXPERTS, N_CTX // LANE, LANE)
        ye = _gather_ffn(offs, pos, h2, N_LAT, i, wg, wu, wd, CTX_LEN, BATCH)
        x_ctx = _scatter(offs, pos, aff3, ye, x1, mods[i], fg, CTX_LEN, N_LAT, lambda b: CTX_GROUP)
```
